```python
import math
import jax, jax.numpy as jnp
from jax import lax
import numpy as np

D_MODEL = 1024
BATCH = 4
SEQ = 4096
DEPTH = 4
DEC_BATCH = 128
DEC_SEQ = 1
PAST_LEN = 2048
PAGE_SIZE = 128

DIFF_HEADS = 4
DIFF_QK_DIM = 32
DIFF_V_DIM = 64
DIFF_WIDTH = 256
SSD_HEADS = 8
SSD_HEAD_DIM = 64
SSD_WIDTH = 512
SSD_GROUPS = 2
SSD_STATE = 128
SSD_CONV = 4
SSD_CHUNK = 128
SSD_CONV_CH = 1024
SB_HEADS = 4
SB_HEAD_DIM = 64
SB_WIDTH = 256
MIX_WIDTH = 1024
D_FF = 2816
Q_BLOCK = 128
RMS_EPS = 1e-6
IN_SIZES = (256, 256, 256, 512, 1024, 8, 256, 256, 256)
D_IN = 3080

kernel_name = "hymba_diff_ssd_stickbreak_macaron_step"


def _rmsnorm(x, g):
    xf = x.astype(jnp.float32)
    y = xf * lax.rsqrt(jnp.mean(xf * xf, axis=-1, keepdims=True) + RMS_EPS)
    return (y * g.astype(jnp.float32)).astype(x.dtype)


def _half_ffn(x, g, w_gu, w_down):
    h = _rmsnorm(x, g) @ w_gu
    gate, up = jnp.split(h, 2, axis=-1)
    return x + 0.5 * ((jax.nn.silu(gate) * up) @ w_down)


def _split_in(p):
    outs, i = [], 0
    for n in IN_SIZES:
        outs.append(p[..., i:i + n])
        i += n
    return outs


def _gather_pages(pool, page_table):
    g = pool[page_table]
    return g.reshape(page_table.shape[0], page_table.shape[1] * pool.shape[1], *pool.shape[2:])


def _diff_attention(q, k, v, lam, q_pos, k_pos):
    s = jnp.einsum('bthmd,bshmd->bmhts', q, k).astype(jnp.float32) * (DIFF_QK_DIM ** -0.5)
    mask = k_pos[None, :] <= q_pos[:, None]
    p = jax.nn.softmax(jnp.where(mask, s, -jnp.inf), axis=-1)
    w = p[:, 0] - lam * p[:, 1]
    return jnp.einsum('bhts,bshe->bthe', w.astype(v.dtype), v)


def _stick_breaking(q, k, v, q_pos, k_pos):
    z = jnp.einsum('bthd,bshd->bhts', q, k).astype(jnp.float32) * (SB_HEAD_DIM ** -0.5)
    mask = k_pos[None, :] < q_pos[:, None]
    log_1m = jnp.where(mask, jax.nn.log_sigmoid(-z), 0.0)
    suffix = lax.cumsum(log_1m, axis=3, reverse=True) - log_1m
    a = jnp.where(mask, jnp.exp(jax.nn.log_sigmoid(z) + suffix), 0.0)
    return jnp.einsum('bhts,bshe->bthe', a.astype(v.dtype), v)


def _query_blocks(fn, q, q_pos):
    b, t = q.shape[:2]
    nb = t // Q_BLOCK
    qb = jnp.moveaxis(q.reshape(b, nb, Q_BLOCK, *q.shape[2:]), 1, 0)
    pb = q_pos.reshape(nb, Q_BLOCK)
    out = lax.map(lambda a: fn(a[0], a[1]), (qb, pb))
    return jnp.moveaxis(out, 0, 1).reshape(b, t, *out.shape[3:])


def _dwconv(xpad, w, bias):
    c = xpad.shape[-1]
    y = lax.conv_general_dilated(xpad, w[:, None, :].astype(xpad.dtype), (1,), 'VALID',
                                 dimension_numbers=('NWC', 'WIO', 'NWC'), feature_group_count=c)
    return y + bias.astype(xpad.dtype)


def _ssd_chunked(x, dt, A, Bm, Cm, h0):
    b, l = x.shape[:2]
    Q, G = SSD_CHUNK, SSD_GROUPS
    E = SSD_HEADS // G
    nc = l // Q
    xc = x.reshape(b, nc, Q, G, E, SSD_HEAD_DIM)
    dtc = dt.reshape(b, nc, Q, G, E)
    xdt = xc * dtc[..., None]
    cum = jnp.cumsum((dt * A).reshape(b, nc, Q, G, E), axis=2)
    Bc = Bm.reshape(b, nc, Q, G, SSD_STATE)
    Cc = Cm.reshape(b, nc, Q, G, SSD_STATE)
    causal = jnp.tril(jnp.ones((Q, Q), dtype=bool))[:, :, None, None]
    seg = cum[:, :, :, None] - cum[:, :, None, :]
    Lmat = jnp.exp(jnp.where(causal, seg, -jnp.inf))
    CB = jnp.einsum('bctgn,bcsgn->bctsg', Cc, Bc)
    y_intra = jnp.einsum('bctsg,bctsge,bcsgep->bctgep', CB, Lmat, xdt)
    decay_end = jnp.exp(cum[:, :, -1:] - cum)
    states = jnp.einsum('bcsgn,bcsge,bcsgep->bcgepn', Bc, decay_end, xdt)
    chunk_decay = jnp.exp(cum[:, :, -1])

    def step(h, inp):
        st, dec = inp
        return dec[..., None, None] * h + st, h

    h0g = h0.reshape(b, G, E, SSD_HEAD_DIM, SSD_STATE)
    hT, h_starts = lax.scan(step, h0g, (jnp.moveaxis(states, 1, 0), jnp.moveaxis(chunk_decay, 1, 0)))
    h_starts = jnp.moveaxis(h_starts, 0, 1)
    y_inter = jnp.einsum('bctgn,bcgepn,bctge->bctgep', Cc, h_starts, jnp.exp(cum))
    y = (y_intra + y_inter).reshape(b, l, SSD_HEADS, SSD_HEAD_DIM)
    return y, hT.reshape(b, SSD_HEADS, SSD_HEAD_DIM, SSD_STATE)


def _ssd_recurrent(x, dt, A, Bm, Cm, h0):
    b = x.shape[0]
    G = SSD_GROUPS
    E = SSD_HEADS // G
    Ag = A.reshape(G, E)

    def step(h, inp):
        xt, dtt, Bt, Ct = inp
        dtg = dtt.reshape(b, G, E)
        dA = jnp.exp(dtg * Ag)
        xg = xt.reshape(b, G, E, SSD_HEAD_DIM) * dtg[..., None]
        h = dA[..., None, None] * h + jnp.einsum('bgep,bgn->bgepn', xg, Bt)
        y = jnp.einsum('bgepn,bgn->bgep', h, Ct).reshape(b, SSD_HEADS, SSD_HEAD_DIM)
        return h, y

    h0g = h0.reshape(b, G, E, SSD_HEAD_DIM, SSD_STATE)
    hT, ys = lax.scan(step, h0g, (jnp.moveaxis(x, 1, 0), jnp.moveaxis(dt, 1, 0),
                                  jnp.moveaxis(Bm, 1, 0), jnp.moveaxis(Cm, 1, 0)))
    return jnp.moveaxis(ys, 0, 1), hT.reshape(b, SSD_HEADS, SSD_HEAD_DIM, SSD_STATE)


def _token_mix(u, lp, lam_init, past):
    b, t, _ = u.shape
    dq, dk, dv, z, xbc, dtr, sq, sk, sv = _split_in(u @ lp['w_in'])
    q_d = dq.reshape(b, t, DIFF_HEADS, 2, DIFF_QK_DIM)
    k_d = dk.reshape(b, t, DIFF_HEADS, 2 * DIFF_QK_DIM)
    v_d = dv.reshape(b, t, DIFF_HEADS, DIFF_V_DIM)
    q_s = sq.reshape(b, t, SB_HEADS, SB_HEAD_DIM)
    k_s = sk.reshape(b, t, SB_HEADS, SB_HEAD_DIM)
    v_s = sv.reshape(b, t, SB_HEADS, SB_HEAD_DIM)
    f32 = jnp.float32
    lam = (jnp.exp(jnp.sum(lp['lq1'].astype(f32) * lp['lk1'].astype(f32)))
           - jnp.exp(jnp.sum(lp['lq2'].astype(f32) * lp['lk2'].astype(f32))) + lam_init)
    if past is None:
        kd_all, vd_all, ks_all, vs_all = k_d, v_d, k_s, v_s
        conv_in = jnp.pad(xbc, ((0, 0), (SSD_CONV - 1, 0), (0, 0)))
        h0 = jnp.zeros((b, SSD_HEADS, SSD_HEAD_DIM, SSD_STATE), f32)
    else:
        kd_past, vd_past, ks_past, vs_past, h0, conv_buf = past
        kd_all = jnp.concatenate([kd_past.astype(k_d.dtype), k_d], axis=1)
        vd_all = jnp.concatenate([vd_past.astype(v_d.dtype), v_d], axis=1)
        ks_all = jnp.concatenate([ks_past.astype(k_s.dtype), k_s], axis=1)
        vs_all = jnp.concatenate([vs_past.astype(v_s.dtype), v_s], axis=1)
        conv_in = jnp.concatenate([conv_buf.astype(xbc.dtype), xbc], axis=1)
    n_keys = kd_all.shape[1]
    k_pos = jnp.arange(n_keys)
    q_pos = k_pos[n_keys - t:]
    kd5 = kd_all.reshape(b, n_keys, DIFF_HEADS, 2, DIFF_QK_DIM)
    diff_fn = lambda qb, pb: _diff_attention(qb, kd5, vd_all, lam, pb, k_pos)
    sb_fn = lambda qb, pb: _stick_breaking(qb, ks_all, vs_all, pb, k_pos)
    if past is None:
        o_d = _query_blocks(diff_fn, q_d, q_pos)
        o_s = _query_blocks(sb_fn, q_s, q_pos)
    else:
        o_d = diff_fn(q_d, q_pos)
        o_s = sb_fn(q_s, q_pos)
    o_d = (_rmsnorm(o_d, lp['diff_norm']) * (1.0 - lam_init)).reshape(b, t, DIFF_WIDTH)
    o_s = _rmsnorm(o_s, lp['sb_norm']).reshape(b, t, SB_WIDTH)
    xbc_c = jax.nn.silu(_dwconv(conv_in, lp['conv_w'], lp['conv_b']))
    gn = SSD_GROUPS * SSD_STATE
    xs = xbc_c[..., :SSD_WIDTH].reshape(b, t, SSD_HEADS, SSD_HEAD_DIM).astype(f32)
    Bm = xbc_c[..., SSD_WIDTH:SSD_WIDTH + gn].reshape(b, t, SSD_GROUPS, SSD_STATE).astype(f32)
    Cm = xbc_c[..., SSD_WIDTH + gn:].reshape(b, t, SSD_GROUPS, SSD_STATE).astype(f32)
    dt = jax.nn.softplus(dtr.astype(f32) + lp['dt_bias'].astype(f32))
    A = -jnp.exp(lp['a_log'].astype(f32))
    ssd_fn = _ssd_chunked if past is None else _ssd_recurrent
    y, h_new = ssd_fn(xs, dt, A, Bm, Cm, h0.astype(f32))
    y = y + lp['ssd_d'].astype(f32)[:, None] * xs
    y = y.reshape(b, t, SSD_WIDTH) * jax.nn.silu(z.astype(f32))
    y = _rmsnorm(y.reshape(b, t, SSD_GROUPS, SSD_WIDTH // SSD_GROUPS),
                 lp['ssd_norm'].reshape(SSD_GROUPS, SSD_WIDTH // SSD_GROUPS))
    y = y.reshape(b, t, SSD_WIDTH).astype(u.dtype)
    out = jnp.concatenate([o_d.astype(u.dtype), y, o_s.astype(u.dtype)], axis=-1) @ lp['w_out']
    new_state = (k_d, v_d, k_s, v_s, h_new.astype(h0.dtype), conv_in[:, conv_in.shape[1] - (SSD_CONV - 1):])
    return out, new_state


def setup_inputs(seed: int = 0) -> dict:
    key = jax.random.key(seed)
    ks = iter(jax.random.split(key, 40))
    nrm = lambda shape, scale: jax.random.normal(next(ks), shape, jnp.float32) * scale
    gain = lambda shape: 1.0 + nrm(shape, 0.02)
    n_pages = PAST_LEN // PAGE_SIZE
    n_used = DEC_BATCH * n_pages
    n_pool = n_used + max(1, n_used // 4)
    page_table = jax.random.permutation(next(ks), n_pool)[:n_used].reshape(DEC_BATCH, n_pages).astype(jnp.int32)
    dt0 = jnp.exp(jax.random.uniform(next(ks), (DEPTH, SSD_HEADS), jnp.float32,
                                     minval=math.log(1e-3), maxval=math.log(1e-1)))
    dt_bias = dt0 + jnp.log(-jnp.expm1(-dt0))
    a_log = jnp.log(jax.random.uniform(next(ks), (DEPTH, SSD_HEADS), jnp.float32, minval=1.0, maxval=16.0))
    kv_d = (DEPTH, n_pool, PAGE_SIZE, DIFF_HEADS, 2 * DIFF_QK_DIM)
    kv_s = (DEPTH, n_pool, PAGE_SIZE, SB_HEADS, SB_HEAD_DIM)
    return {
        'x_prompt': nrm((BATCH, SEQ, D_MODEL), 1.0),
        'x_sample': nrm((DEC_BATCH, DEC_SEQ, D_MODEL), 1.0),
        'cache_diff_k': nrm(kv_d, 1.0),
        'cache_diff_v': nrm((DEPTH, n_pool, PAGE_SIZE, DIFF_HEADS, DIFF_V_DIM), 1.0),
        'cache_sb_k': nrm(kv_s, 1.0),
        'cache_sb_v': nrm(kv_s, 1.0),
        'state_ssm': nrm((DEPTH, DEC_BATCH, SSD_HEADS, SSD_HEAD_DIM, SSD_STATE), 0.1),
        'state_conv': nrm((DEPTH, DEC_BATCH, SSD_CONV - 1, SSD_CONV_CH), 1.0),
        'page_table': page_table,
        'norm_ffn1': gain((DEPTH, D_MODEL)),
        'w_ffn1_gu': nrm((DEPTH, D_MODEL, 2 * D_FF), D_MODEL ** -0.5),
        'w_ffn1_down': nrm((DEPTH, D_FF, D_MODEL), D_FF ** -0.5),
        'norm_mix': gain((DEPTH, D_MODEL)),
        'w_in': nrm((DEPTH, D_MODEL, D_IN), D_MODEL ** -0.5),
        'diff_lambda_q1': nrm((DEPTH, DIFF_QK_DIM), 0.1),
        'diff_lambda_k1': nrm((DEPTH, DIFF_QK_DIM), 0.1),
        'diff_lambda_q2': nrm((DEPTH, DIFF_QK_DIM), 0.1),
        'diff_lambda_k2': nrm((DEPTH, DIFF_QK_DIM), 0.1),
        'diff_norm': gain((DEPTH, DIFF_V_DIM)),
        'ssd_conv_w': nrm((DEPTH, SSD_CONV, SSD_CONV_CH), SSD_CONV ** -0.5),
        'ssd_conv_b': nrm((DEPTH, SSD_CONV_CH), 0.01),
        'ssd_dt_bias': dt_bias,
        'ssd_a_log': a_log,
        'ssd_d': gain((DEPTH, SSD_HEADS)),
        'ssd_norm': gain((DEPTH, SSD_WIDTH)),
        'sb_norm': gain((DEPTH, SB_HEAD_DIM)),
        'w_out': nrm((DEPTH, MIX_WIDTH, D_MODEL), MIX_WIDTH ** -0.5),
        'norm_ffn2': gain((DEPTH, D_MODEL)),
        'w_ffn2_gu': nrm((DEPTH, D_MODEL, 2 * D_FF), D_MODEL ** -0.5),
        'w_ffn2_down': nrm((DEPTH, D_FF, D_MODEL), D_FF ** -0.5),
        'norm_final': gain((D_MODEL,)),
    }


def reference(x_prompt, x_sample, cache_diff_k, cache_diff_v, cache_sb_k, cache_sb_v, state_ssm, state_conv,
              page_table, norm_ffn1, w_ffn1_gu, w_ffn1_down, norm_mix, w_in, diff_lambda_q1, diff_lambda_k1,
              diff_lambda_q2, diff_lambda_k2, diff_norm, ssd_conv_w, ssd_conv_b, ssd_dt_bias, ssd_a_log, ssd_d,
              ssd_norm, sb_norm, w_out, norm_ffn2, w_ffn2_gu, w_ffn2_down, norm_final):
    xp, xs = x_prompt, x_sample
    new_p, new_s = [], []
    for l in range(DEPTH):
        lam_init = 0.8 - 0.6 * math.exp(-0.3 * l)
        lp = {'w_in': w_in[l], 'lq1': diff_lambda_q1[l], 'lk1': diff_lambda_k1[l],
              'lq2': diff_lambda_q2[l], 'lk2': diff_lambda_k2[l], 'diff_norm': diff_norm[l],
              'conv_w': ssd_conv_w[l], 'conv_b': ssd_conv_b[l], 'dt_bias': ssd_dt_bias[l],
              'a_log': ssd_a_log[l], 'ssd_d': ssd_d[l], 'ssd_norm': ssd_norm[l],
              'sb_norm': sb_norm[l], 'w_out': w_out[l]}
        xp = _half_ffn(xp, norm_ffn1[l], w_ffn1_gu[l], w_ffn1_down[l])
        xs = _half_ffn(xs, norm_ffn1[l], w_ffn1_gu[l], w_ffn1_down[l])
        past = (_gather_pages(cache_diff_k[l], page_table), _gather_pages(cache_diff_v[l], page_table),
                _gather_pages(cache_sb_k[l], page_table), _gather_pages(cache_sb_v[l], page_table),
                state_ssm[l], state_conv[l])
        mp, sp = _token_mix(_rmsnorm(xp, norm_mix[l]), lp, lam_init, None)
        ms, ss = _token_mix(_rmsnorm(xs, norm_mix[l]), lp, lam_init, past)
        xp = xp + mp
        xs = xs + ms
        xp = _half_ffn(xp, norm_ffn2[l], w_ffn2_gu[l], w_ffn2_down[l])
        xs = _half_ffn(xs, norm_ffn2[l], w_ffn2_gu[l], w_ffn2_down[l])
        new_p.append(sp)
        new_s.append(ss)
    y_prompt = _rmsnorm(xp, norm_final)
    y_sample = _rmsnorm(xs, norm_final)
    stk = lambda states, i: jnp.stack([st[i] for st in states], axis=0)
    return (y_prompt, y_sample,
            stk(new_p, 0), stk(new_p, 1), stk(new_p, 2), stk(new_p, 3), stk(new_p, 4), stk(new_p, 5),
            stk(new_s, 0), stk(new_s, 1), stk(new_s, 2), stk(new_s, 3), stk(new_s, 4), stk(new_s, 5))
```

```python
import functools
import math

import jax
import jax.numpy as jnp
from jax import lax
from jax.experimental import pallas as pl
from jax.experimental.pallas import tpu as pltpu

F32, BF16 = jnp.float32, jnp.bfloat16
HIGHEST = lax.Precision.HIGHEST

D_MODEL = 1024
D_FF = 2816
DIFF_QK = 32
HEAD_W = 64
ATT_W = 256
SSD_W = 512
SSD_HEADS = 8
SSD_STATE = 128
SSD_CHUNK = 128
CONV_K = 4
CONV_CH = 1024
EPS = 1e-6
NEG = -1e30
IN_SIZES = (256, 256, 256, 512, 1024, 8, 256, 256, 256)
DT_PAD = 128
FF_CHUNK = 256
VMEM_LIMIT = 56 * 1024 * 1024


def _cparams(*sem):
    return pltpu.CompilerParams(dimension_semantics=sem, vmem_limit_bytes=VMEM_LIMIT)


def _rms(x, g):
    return x * lax.rsqrt(jnp.mean(x * x, axis=-1, keepdims=True) + EPS) * g


def _softplus(x):
    return jnp.maximum(x, 0.0) + jnp.log1p(jnp.exp(-jnp.abs(x)))


def _dot(a, b):
    return jnp.dot(a, b, preferred_element_type=F32)


def _dot_nt(a, b):
    return lax.dot_general(a, b, (((1,), (1,)), ((), ())), preferred_element_type=F32)


def _dot_f32(a, b):
    return jnp.dot(a, b, precision=HIGHEST, preferred_element_type=F32)


def _split2(x):
    hi = x.astype(BF16)
    lo = (x - hi.astype(F32)).astype(BF16)
    return hi, lo


def _const_spec(shape):
    nd = len(shape)
    return pl.BlockSpec(shape, lambda *_: (0,) * nd, pipeline_mode=pl.Buffered(1))


def _head_rms(o, gain, scale):
    i = lax.broadcasted_iota(jnp.int32, (ATT_W, ATT_W), 0) >> 6
    j = lax.broadcasted_iota(jnp.int32, (ATT_W, ATT_W), 1) >> 6
    seg = jnp.where(i == j, 1.0 / HEAD_W, 0.0).astype(F32)
    ms = _dot_f32(o * o, seg)
    return o * lax.rsqrt(ms + EPS) * gain * scale


def _ffn_body(*refs, n_chunks, final):
    if final:
        x_ref, g_ref, wg_ref, wu_ref, wd_ref, gf_ref, o_ref = refs
    else:
        x_ref, g_ref, wg_ref, wu_ref, wd_ref, o_ref = refs
    x = x_ref[...]
    xn = _rms(x, g_ref[...]).astype(BF16)
    acc = jnp.zeros_like(x)
    for c in range(n_chunks):
        sl = slice(c * FF_CHUNK, (c + 1) * FF_CHUNK)
        gate = _dot(xn, wg_ref[:, sl])
        up = _dot(xn, wu_ref[:, sl])
        h = (gate * jax.nn.sigmoid(gate) * up).astype(BF16)
        acc = acc + _dot(h, wd_ref[sl, :])
    y = x + 0.5 * acc
    if final:
        y = _rms(y, gf_ref[...])
    o_ref[...] = y


def _ffn(x, g, wg, wu, wd, final_g=None):
    t = x.shape[0]
    tm = min(512, t)
    assert t % tm == 0 and D_FF % FF_CHUNK == 0
    row = pl.BlockSpec((tm, D_MODEL), lambda i: (i, 0))
    in_specs = [row, _const_spec((1, D_MODEL)), _const_spec((D_MODEL, D_FF)), _const_spec((D_MODEL, D_FF)),
                _const_spec((D_FF, D_MODEL))]
    args = [x, g, wg, wu, wd]
    if final_g is not None:
        in_specs.append(_const_spec((1, D_MODEL)))
        args.append(final_g)
    return pl.pallas_call(
        functools.partial(_ffn_body, n_chunks=D_FF // FF_CHUNK, final=final_g is not None),
        grid=(t // tm,), in_specs=in_specs, out_specs=row,
        out_shape=jax.ShapeDtypeStruct((t, D_MODEL), F32),
        compiler_params=_cparams("parallel"), name="ffn")(*args)


_NAT_W = (256, 512, 1024, 256, DT_PAD, 256, 256, 256)
_N_F32 = 5
_N_T = 4
ATT_TK = 256


def _inproj_body(x_ref, g_ref, wn_ref, wt_ref, *o_refs, sample, tk):
    u = _rms(x_ref[...], g_ref[...]).astype(BF16)
    offs = [sum(_NAT_W[:i]) for i in range(len(_NAT_W))]
    nat = lambda i: _dot(u, wn_ref[:, offs[i]:offs[i] + _NAT_W[i]])
    for i in range(_N_F32):
        o_refs[i][...] = nat(i)
    t_refs = o_refs[_N_F32:_N_F32 + _N_T]
    kv_t = [_dot_nt(wt_ref[j * ATT_W:(j + 1) * ATT_W, :], u) for j in range(_N_T)]
    for r, v in zip(t_refs, kv_t):
        r[...] = v
    rest = o_refs[_N_F32 + _N_T:]
    if sample:
        rest[0][...] = nat(7)
        rest[1][...] = nat(5)
    else:
        dk_att, sk_att, dv_att, sv_att = rest
        for r, v in ((dk_att, kv_t[0]), (sk_att, kv_t[2])):
            for c in range(v.shape[1] // tk):
                r[c] = v[:, c * tk:(c + 1) * tk].astype(BF16)
        dv_att[...] = nat(5).astype(BF16)
        sv_att[...] = nat(6).astype(BF16)


def _inproj(x, g, w_nat, w_t, batch, seq, sample):
    t = x.shape[0]
    tm = min(512, seq)
    tk = min(ATT_TK, seq)
    assert seq % tm == 0 and tm % tk == 0
    nj = seq // tm
    rows = lambda w: pl.BlockSpec((tm, w), lambda i: (i, 0))
    out_specs = [rows(w) for w in _NAT_W[:_N_F32]]
    out_shape = [jax.ShapeDtypeStruct((t, w), F32) for w in _NAT_W[:_N_F32]]
    out_specs += [pl.BlockSpec((None, ATT_W, tm), lambda i: (i // nj, 0, i % nj))] * _N_T
    out_shape += [jax.ShapeDtypeStruct((batch, ATT_W, seq), F32)] * _N_T
    if sample:
        out_specs += [rows(ATT_W)] * 2
        out_shape += [jax.ShapeDtypeStruct((t, ATT_W), F32)] * 2
    else:
        out_specs += [pl.BlockSpec((None, tm // tk, ATT_W, tk), lambda i: (i // nj, i % nj, 0, 0))] * 2
        out_shape += [jax.ShapeDtypeStruct((batch, seq // tk, ATT_W, tk), BF16)] * 2
        out_specs += [rows(ATT_W)] * 2
        out_shape += [jax.ShapeDtypeStruct((t, ATT_W), BF16)] * 2
    return pl.pallas_call(
        functools.partial(_inproj_body, sample=sample, tk=tk), grid=(t // tm,),
        in_specs=[rows(D_MODEL), _const_spec((1, D_MODEL)), _const_spec(w_nat.shape), _const_spec(w_t.shape)],
        out_specs=out_specs, out_shape=out_shape,
        compiler_params=_cparams("parallel"), name="inproj")(x, g, w_nat, w_t)


def _outproj_body(x_ref, od_ref, y_ref, os_ref, w_ref, o_ref):
    acc = _dot(od_ref[...].astype(BF16), w_ref[0:ATT_W, :])
    acc = acc + _dot(y_ref[...].astype(BF16), w_ref[ATT_W:ATT_W + SSD_W, :])
    acc = acc + _dot(os_ref[...].astype(BF16), w_ref[ATT_W + SSD_W:, :])
    o_ref[...] = x_ref[...] + acc


def _outproj(x, od, y, os_, w):
    t = x.shape[0]
    tm = min(512, t)
    spec = lambda w_: pl.BlockSpec((tm, w_), lambda i: (i, 0))
    return pl.pallas_call(
        _outproj_body, grid=(t // tm,),
        in_specs=[spec(D_MODEL), spec(ATT_W), spec(SSD_W), spec(ATT_W), _const_spec((D_MODEL, D_MODEL))],
        out_specs=spec(D_MODEL), out_shape=jax.ShapeDtypeStruct((t, D_MODEL), F32),
        compiler_params=_cparams("parallel"), name="outproj")(x, od, y, os_, w)


def _lam(lq1, lk1, lq2, lk2, lam_init):
    s1 = jnp.sum(lq1[...] * lk1[...], axis=-1, keepdims=True)
    s2 = jnp.sum(lq2[...] * lk2[...], axis=-1, keepdims=True)
    return jnp.exp(s1) - jnp.exp(s2) + lam_init


def _head_masks():
    hl = lax.broadcasted_iota(jnp.int32, (16, ATT_W), 1) >> 6
    return [jnp.where(hl == h, 1.0, 0.0).astype(BF16) for h in range(4)]


def _masked_v(vblk, masks):
    v3 = vblk.reshape(vblk.shape[0] // 16, 16, ATT_W)
    return [(v3 * m[None]).reshape(vblk.shape) for m in masks]


def _diff_attn_body(lq1, lk1, lq2, lk2, gn_ref, q_ref, k_ref, v_ref, o_ref,
                    qs_ref, m_ref, l_ref, acc_ref, *, tq, lam_init):
    qi = pl.program_id(1)
    rows = 8 * tq
    q = q_ref[...] * (DIFF_QK ** -0.5)
    seg = lax.broadcasted_iota(jnp.int32, (tq, ATT_W), 1) >> 5
    for j in range(8):
        qs_ref[j * tq:(j + 1) * tq, :] = jnp.where(seg == j, q, 0.0).astype(BF16)
    m_ref[...] = jnp.full((rows, 1), NEG, F32)
    l_ref[...] = jnp.zeros((rows, 1), F32)
    acc_ref[...] = jnp.zeros((2, tq, ATT_W), F32)
    head_lane = lax.broadcasted_iota(jnp.int32, (1, ATT_W), 1) >> 6
    masks = _head_masks()

    def expand(col, mi):
        out = jnp.broadcast_to(col[mi * tq:(mi + 1) * tq, :], (tq, ATT_W))
        for h in range(1, 4):
            r0 = (2 * h + mi) * tq
            out = jnp.where(head_lane == h, col[r0:r0 + tq, :], out)
        return out

    def step(kj, masked):
        start = pl.multiple_of(kj * tq, tq)
        vh = _masked_v(v_ref[pl.ds(start, tq), :], masks)
        s = _dot(qs_ref[...], k_ref[kj])
        if masked:
            row = lax.broadcasted_iota(jnp.int32, (rows, tq), 0) & (tq - 1)
            col = lax.broadcasted_iota(jnp.int32, (rows, tq), 1)
            s = jnp.where(col <= row, s, NEG)
        m_prev = m_ref[...]
        m_new = jnp.maximum(m_prev, jnp.max(s, axis=-1, keepdims=True))
        alpha = jnp.exp(m_prev - m_new)
        p = jnp.exp(s - m_new)
        l_ref[...] = alpha * l_ref[...] + jnp.sum(p, axis=-1, keepdims=True)
        m_ref[...] = m_new
        pb = p.astype(BF16)
        for mi in range(2):
            pv = None
            for h in range(4):
                r0 = (2 * h + mi) * tq
                d = _dot(pb[r0:r0 + tq, :], vh[h])
                pv = d if pv is None else pv + d
            acc_ref[mi] = acc_ref[mi] * expand(alpha, mi) + pv

    def loop_body(kj, carry):
        step(kj, False)
        return carry

    lax.fori_loop(0, qi, loop_body, 0)
    step(qi, True)

    lam = _lam(lq1, lk1, lq2, lk2, lam_init)
    linv = 1.0 / l_ref[...]
    o = acc_ref[0] * expand(linv, 0) - lam * (acc_ref[1] * expand(linv, 1))
    o_ref[...] = _head_rms(o, gn_ref[...], 1.0 - lam_init)


def _diff_attn(q, k, v, lam_params, gain, batch, seq, lam_init):
    tq = min(ATT_TK, seq)
    assert seq % tq == 0 and tq & (tq - 1) == 0
    nq = seq // tq
    small = lambda a: pl.BlockSpec(a.shape, lambda b, i: (0, 0))
    return pl.pallas_call(
        functools.partial(_diff_attn_body, tq=tq, lam_init=lam_init),
        grid=(batch, nq),
        in_specs=[small(a) for a in lam_params] + [small(gain)] + [
            pl.BlockSpec((tq, ATT_W), lambda b, i: (b * nq + i, 0)),
            pl.BlockSpec((None, nq, ATT_W, tq), lambda b, i: (b, 0, 0, 0)),
            pl.BlockSpec((seq, ATT_W), lambda b, i: (b, 0))],
        out_specs=pl.BlockSpec((tq, ATT_W), lambda b, i: (b * nq + i, 0)),
        out_shape=jax.ShapeDtypeStruct((batch * seq, ATT_W), F32),
        scratch_shapes=[pltpu.VMEM((8 * tq, ATT_W), BF16), pltpu.VMEM((8 * tq, 1), F32),
                        pltpu.VMEM((8 * tq, 1), F32), pltpu.VMEM((2, tq, ATT_W), F32)],
        compiler_params=_cparams("parallel", "arbitrary"), name="diff_attn")(*lam_params, gain, q, k, v)


def _sb_attn_body(gn_ref, q_ref, k_ref, v_ref, o_ref, qs_ref, c_ref, acc_ref, *, tq):
    qi = pl.program_id(1)
    rows = 4 * tq
    q = q_ref[...] * (HEAD_W ** -0.5)
    hq = lax.broadcasted_iota(jnp.int32, (tq, ATT_W), 1) >> 6
    for h in range(4):
        qs_ref[h * tq:(h + 1) * tq, :] = jnp.where(hq == h, q, 0.0).astype(BF16)
    c_ref[...] = jnp.zeros((rows, 1), F32)
    acc_ref[...] = jnp.zeros((tq, ATT_W), F32)
    masks = _head_masks()
    jj = lax.broadcasted_iota(jnp.int32, (tq, tq), 0)
    ss = lax.broadcasted_iota(jnp.int32, (tq, tq), 1)
    uincl = jnp.where(jj >= ss, 1.0, 0.0).astype(BF16)

    def step(kj, masked):
        start = pl.multiple_of(kj * tq, tq)
        vh = _masked_v(v_ref[pl.ds(start, tq), :], masks)
        z = _dot(qs_ref[...], k_ref[kj])
        sp = _softplus(z)
        log_1m = -sp
        if masked:
            row = lax.broadcasted_iota(jnp.int32, (rows, tq), 0) & (tq - 1)
            col = lax.broadcasted_iota(jnp.int32, (rows, tq), 1)
            mask = col < row
            log_1m = jnp.where(mask, log_1m, 0.0)
        hi, lo = _split2(log_1m)
        incl = _dot(hi, uincl) + _dot(lo, uincl)
        c = c_ref[...]
        a = jnp.exp(z - sp + (c + (incl - log_1m)))
        if masked:
            a = jnp.where(mask, a, 0.0)
        c_ref[...] = c + jnp.sum(log_1m, axis=-1, keepdims=True)
        ab = a.astype(BF16)
        pv = None
        for h in range(4):
            d = _dot(ab[h * tq:(h + 1) * tq, :], vh[h])
            pv = d if pv is None else pv + d
        acc_ref[...] = acc_ref[...] + pv

    step(qi, True)

    def loop_body(i, carry):
        step(qi - 1 - i, False)
        return carry

    lax.fori_loop(0, qi, loop_body, 0)
    o_ref[...] = _head_rms(acc_ref[...], gn_ref[...], 1.0)


def _sb_attn(q, k, v, gain, batch, seq):
    tq = min(ATT_TK, seq)
    assert seq % tq == 0 and tq & (tq - 1) == 0
    nq = seq // tq
    return pl.pallas_call(
        functools.partial(_sb_attn_body, tq=tq),
        grid=(batch, nq),
        in_specs=[pl.BlockSpec(gain.shape, lambda b, i: (0, 0)),
                  pl.BlockSpec((tq, ATT_W), lambda b, i: (b * nq + i, 0)),
                  pl.BlockSpec((None, nq, ATT_W, tq), lambda b, i: (b, 0, 0, 0)),
                  pl.BlockSpec((seq, ATT_W), lambda b, i: (b, 0))],
        out_specs=pl.BlockSpec((tq, ATT_W), lambda b, i: (b * nq + i, 0)),
        out_shape=jax.ShapeDtypeStruct((batch * seq, ATT_W), F32),
        scratch_shapes=[pltpu.VMEM((4 * tq, ATT_W), BF16), pltpu.VMEM((4 * tq, 1), F32),
                        pltpu.VMEM((tq, ATT_W), F32)],
        compiler_params=_cparams("parallel", "arbitrary"), name="sb_attn")(gain, q, k, v)


def _expand_mat():
    r = lax.broadcasted_iota(jnp.int32, (DT_PAD, SSD_W), 0)
    c = lax.broadcasted_iota(jnp.int32, (DT_PAD, SSD_W), 1) >> 6
    return jnp.where(r == c, 1.0, 0.0).astype(F32)


def _a_row(alog_ref):
    lane = lax.broadcasted_iota(jnp.int32, (1, DT_PAD), 1)
    return jnp.where(lane < SSD_HEADS, -jnp.exp(alog_ref[...]), 0.0)


def _gate_norm(y, z, nw):
    y = y * (z * jax.nn.sigmoid(z))
    half = SSD_W // 2
    parts = [_rms(y[:, g * half:(g + 1) * half], nw[:, g * half:(g + 1) * half]) for g in range(2)]
    return jnp.concatenate(parts, axis=1)


def _ssd_body(z_ref, xbc_ref, dt_ref, cw_ref, cb_ref, dtb_ref, alog_ref, dfull_ref, nw_ref,
              y_ref, hout_ref, cout_ref, ext_ref, ht_ref):
    c = pl.program_id(1)
    q = SSD_CHUNK

    @pl.when(c == 0)
    def _():
        ext_ref[0:8, :] = jnp.zeros((8, CONV_CH), F32)
        ht_ref[...] = jnp.zeros_like(ht_ref)

    ext_ref[8:8 + q, :] = xbc_ref[...]
    conv = cb_ref[...]
    for k in range(CONV_K):
        conv = conv + cw_ref[k:k + 1, :] * ext_ref[5 + k:5 + k + q, :]
    tail = ext_ref[q + 5:q + 8, :]
    ext_ref[5:8, :] = tail
    xc = conv * jax.nn.sigmoid(conv)
    xs = xc[:, :SSD_W]
    dt = _softplus(dt_ref[...] + dtb_ref[...])
    dta = dt * _a_row(alog_ref)
    ti = lax.broadcasted_iota(jnp.int32, (q, q), 0)
    si = lax.broadcasted_iota(jnp.int32, (q, q), 1)
    causal = ti >= si
    cum = _dot_f32(jnp.where(causal, 1.0, 0.0).astype(F32), dta)
    emat = _expand_mat()
    cum_full = _dot_f32(cum, emat)
    xdt = xs * _dot_f32(dt, emat)
    cum_last = cum_full[q - 1:q, :]
    xdec = xdt * jnp.exp(cum_last - cum_full)
    exp_cum = jnp.exp(cum_full)
    cum_t = cum.T
    lane = lax.broadcasted_iota(jnp.int32, (1, 128), 1)
    y_slabs = []
    for g in range(2):
        bm = xc[:, SSD_W + g * SSD_STATE:SSD_W + (g + 1) * SSD_STATE]
        cm = xc[:, SSD_W + 2 * SSD_STATE + g * SSD_STATE:SSD_W + 2 * SSD_STATE + (g + 1) * SSD_STATE]
        bmb, cmb = bm.astype(BF16), cm.astype(BF16)
        cb = _dot_nt(cmb, bmb)
        bm_t = bm.T.astype(BF16)
        for j in (2 * g, 2 * g + 1):
            sl = slice(j * 128, (j + 1) * 128)
            xdt_s = xdt[:, sl]
            y_in = None
            for hh in range(2):
                h = 2 * j + hh
                segd = cum[:, h:h + 1] - cum_t[h:h + 1, :]
                w = (cb * jnp.exp(jnp.where(causal, segd, NEG))).astype(BF16)
                xh = jnp.where((lane >> 6) == hh, xdt_s, 0.0).astype(BF16)
                d = _dot(w, xh)
                y_in = d if y_in is None else y_in + d
            ht = ht_ref[j]
            y_x = _dot(cmb, ht.astype(BF16)) * exp_cum[:, sl]
            st = _dot(bm_t, xdec[:, sl].astype(BF16))
            ht_ref[j] = jnp.exp(cum_last[:, sl]) * ht + st
            y_slabs.append(y_in + y_x + dfull_ref[:, sl] * xs[:, sl])
    y = jnp.concatenate(y_slabs, axis=1)
    y_ref[...] = _gate_norm(y, z_ref[...], nw_ref[...])

    @pl.when(c == pl.num_programs(1) - 1)
    def _():
        for j in range(4):
            hout_ref[0, 2 * j:2 * j + 2] = ht_ref[j].T.reshape(2, HEAD_W, SSD_STATE)
        cout_ref[0] = tail


def _ssd_prompt(z, xbc, dt, cw, cb, dtb, alog, dfull, nw, batch, seq):
    q = SSD_CHUNK
    nc = seq // q
    rowspec = lambda w: pl.BlockSpec((q, w), lambda b, c: (b * nc + c, 0))
    small = lambda a: pl.BlockSpec(a.shape, lambda b, c: (0, 0))
    return pl.pallas_call(
        _ssd_body, grid=(batch, nc),
        in_specs=[rowspec(SSD_W), rowspec(CONV_CH), rowspec(DT_PAD)] + [small(a) for a in (cw, cb, dtb, alog, dfull, nw)],
        out_specs=[rowspec(SSD_W),
                   pl.BlockSpec((1, SSD_HEADS, HEAD_W, SSD_STATE), lambda b, c: (b, 0, 0, 0)),
                   pl.BlockSpec((1, CONV_K - 1, CONV_CH), lambda b, c: (b, 0, 0))],
        out_shape=[jax.ShapeDtypeStruct((batch * seq, SSD_W), F32),
                   jax.ShapeDtypeStruct((batch, SSD_HEADS, HEAD_W, SSD_STATE), F32),
                   jax.ShapeDtypeStruct((batch, CONV_K - 1, CONV_CH), F32)],
        scratch_shapes=[pltpu.VMEM((q + 8, CONV_CH), F32), pltpu.VMEM((4, SSD_STATE, 128), F32)],
        compiler_params=_cparams("parallel", "arbitrary"), name="ssd_prompt")(z, xbc, dt, cw, cb, dtb, alog, dfull, nw)


_SB = 8


def _dec_ssd_body(z_ref, xbc_ref, dt_ref, cs_ref, h_ref, cw_ref, cb_ref, dtb_ref, alog_ref, dfull_ref, nw_ref,
                  y_ref, hout_ref, cout_ref, yrow_ref):
    x_new = xbc_ref[...]
    c0, c1, c2 = cs_ref[0], cs_ref[1], cs_ref[2]
    conv = cb_ref[...] + cw_ref[0:1, :] * c0 + cw_ref[1:2, :] * c1 + cw_ref[2:3, :] * c2 + cw_ref[3:4, :] * x_new
    cout_ref[0] = c1
    cout_ref[1] = c2
    cout_ref[2] = x_new
    xc = conv * jax.nn.sigmoid(conv)
    xs = xc[:, :SSD_W]
    bmat = xc[:, SSD_W:SSD_W + 2 * SSD_STATE]
    cmat = xc[:, SSD_W + 2 * SSD_STATE:]
    dt = _softplus(dt_ref[...] + dtb_ref[...])
    da = jnp.exp(dt * _a_row(alog_ref))
    emat = _expand_mat()
    xdt = xs * _dot_f32(dt, emat)
    da_full = _dot_f32(da, emat)
    x_hi, x_lo = _split2(xdt)
    d_hi = da_full.astype(BF16)
    d_r = da_full - d_hi.astype(F32)
    d_mid = d_r.astype(BF16)
    d_lo = (d_r - d_mid.astype(F32)).astype(BF16)
    g0 = lax.broadcasted_iota(jnp.int32, (_SB, SSD_W), 1) < SSD_W // 2
    zx = jnp.zeros((_SB, SSD_W), F32)
    f = lambda a: a.astype(F32)
    x0h, x0l = jnp.where(g0, f(x_hi), 0.0), jnp.where(g0, f(x_lo), 0.0)
    x1h, x1l = jnp.where(g0, 0.0, f(x_hi)), jnp.where(g0, 0.0, f(x_lo))
    xrows = [x0h, x0h, x0l, x1h, x1h, x1l, f(d_hi), f(d_mid), f(d_lo)] + [zx] * 7
    xt = jnp.concatenate(xrows, axis=0).T.astype(BF16)
    b_hi, b_lo = (f(a) for a in _split2(bmat))
    zb = jnp.zeros((_SB, SSD_STATE), F32)
    ob = jnp.ones((_SB, SSD_STATE), F32)
    cat = lambda a, b: jnp.concatenate([a, b], axis=1)
    b0h, b0l, b1h, b1l = b_hi[:, :128], b_lo[:, :128], b_hi[:, 128:], b_lo[:, 128:]
    rrows = [cat(b0h, zb), cat(b0l, zb), cat(b0h, zb), cat(b1h, zb), cat(b1l, zb), cat(b1h, zb),
             cat(zb, ob), cat(zb, ob), cat(zb, ob)] + [cat(zb, zb)] * 7
    r_all = jnp.concatenate(rrows, axis=0)
    rowi = lax.broadcasted_iota(jnp.int32, (16 * _SB, 2 * SSD_STATE), 0) & (_SB - 1)
    row8 = lax.broadcasted_iota(jnp.int32, (8, SSD_STATE), 0)
    lane512 = lax.broadcasted_iota(jnp.int32, (1, SSD_W), 1)
    for i in range(_SB):
        out = _dot(xt, jnp.where(rowi == i, r_all, 0.0).astype(BF16))
        h_new = out[:, SSD_STATE:] * h_ref[i].reshape(SSD_W, SSD_STATE) + out[:, :SSD_STATE]
        hout_ref[i] = h_new.reshape(SSD_HEADS, HEAD_W, SSD_STATE)
        crow = jnp.where(row8 == 0, cmat[i:i + 1, :SSD_STATE],
                         jnp.where(row8 == 1, cmat[i:i + 1, SSD_STATE:], 0.0))
        yy = _dot_nt(crow.astype(BF16), h_new.astype(BF16))
        yrow_ref[i:i + 1, :] = jnp.where(lane512 < SSD_W // 2, yy[0:1, :], yy[1:2, :])
    y = yrow_ref[...] + dfull_ref[...] * xs
    y_ref[...] = _gate_norm(y, z_ref[...], nw_ref[...])


def _ssd_sample(z, xbc, dt, conv_state, ssm_state, layer, cw, cb, dtb, alog, dfull, nw):
    nb = z.shape[0]
    assert nb % _SB == 0
    rowspec = lambda w: pl.BlockSpec((_SB, w), lambda i: (i, 0))
    small = lambda a: pl.BlockSpec(a.shape, lambda i: (0, 0))
    cshape, hshape = (CONV_K - 1, _SB, CONV_CH), (_SB, SSD_HEADS, HEAD_W, SSD_STATE)
    cspec = pl.BlockSpec(cshape, lambda i: (0, i, 0))
    hspec = pl.BlockSpec(hshape, lambda i: (i, 0, 0, 0))
    cin = pl.BlockSpec((None,) + cshape, lambda i: (layer, 0, i, 0))
    hin = pl.BlockSpec((None,) + hshape, lambda i: (layer, i, 0, 0, 0))
    return pl.pallas_call(
        _dec_ssd_body, grid=(nb // _SB,),
        in_specs=[rowspec(SSD_W), rowspec(CONV_CH), rowspec(DT_PAD), cin, hin] + [small(a) for a in (cw, cb, dtb, alog, dfull, nw)],
        out_specs=[rowspec(SSD_W), hspec, cspec],
        out_shape=[jax.ShapeDtypeStruct((nb, SSD_W), F32),
                   jax.ShapeDtypeStruct(ssm_state.shape[1:], F32),
                   jax.ShapeDtypeStruct(conv_state.shape[1:], F32)],
        scratch_shapes=[pltpu.VMEM((_SB, SSD_W), F32)],
        compiler_params=_cparams("parallel"), name="ssd_sample")(z, xbc, dt, conv_state, ssm_state, cw, cb, dtb, alog, dfull, nw)


def _row_head_rms(res, gain, scale):
    hl = lax.broadcasted_iota(jnp.int32, (1, ATT_W), 1) >> 6
    sq = res * res
    ms = jnp.zeros_like(res)
    for h in range(4):
        ms_h = jnp.sum(jnp.where(hl == h, sq, 0.0), axis=-1, keepdims=True) * (1.0 / HEAD_W)
        ms = jnp.where(hl == h, ms_h, ms)
    return res * lax.rsqrt(ms + EPS) * gain * scale


def _dec_attn_body(pt_ref, lq1, lk1, lq2, lk2, gnd_ref, gns_ref, qd_ref, kdn_ref, vdn_ref, qs_ref, *rest,
                   n_pages, page, lam_init):
    del pt_ref
    kd, vd = rest[0:n_pages], rest[n_pages:2 * n_pages]
    ks, vs = rest[2 * n_pages:3 * n_pages], rest[3 * n_pages:4 * n_pages]
    od_ref, os_ref = rest[4 * n_pages:]
    lane8 = lax.broadcasted_iota(jnp.int32, (8, ATT_W), 1)
    row8 = lax.broadcasted_iota(jnp.int32, (8, ATT_W), 0)

    qd = qd_ref[0] * (DIFF_QK ** -0.5)
    qm = jnp.where((lane8 >> 5) == row8, qd, 0.0)
    qmb = qm.astype(BF16)
    s_pages = [_dot(qmb, kd[p][...].astype(BF16)) for p in range(n_pages)]
    s_new = jnp.sum(qm * kdn_ref[0], axis=-1, keepdims=True)
    m = s_new
    for s in s_pages:
        m = jnp.maximum(m, jnp.max(s, axis=-1, keepdims=True))
    l = jnp.exp(s_new - m)
    o = l * vdn_ref[0]
    for p in range(n_pages):
        pr = jnp.exp(s_pages[p] - m)
        l = l + jnp.sum(pr, axis=-1, keepdims=True)
        o = o + _dot_nt(pr.astype(BF16), vd[p][...].astype(BF16))
    o = o / l
    lam = _lam(lq1, lk1, lq2, lk2, lam_init)
    coef = jnp.where((lane8 >> 6) == (row8 >> 1), jnp.where((row8 & 1) == 0, 1.0, -lam), 0.0)
    res = jnp.sum(coef * o, axis=0, keepdims=True)
    od_ref[0] = _row_head_rms(res, gnd_ref[...], 1.0 - lam_init)

    qs = qs_ref[0] * (HEAD_W ** -0.5)
    qsm = jnp.where((lane8 >> 6) == row8, qs, 0.0).astype(BF16)
    z = jnp.concatenate([_dot(qsm, ks[p][...].astype(BF16)) for p in range(n_pages)], axis=0)
    r = 8 * n_pages
    sp = _softplus(z)
    log_1m = -sp
    jj = lax.broadcasted_iota(jnp.int32, (page, page), 0)
    ss = lax.broadcasted_iota(jnp.int32, (page, page), 1)
    uincl = jnp.where(jj >= ss, 1.0, 0.0).astype(BF16)
    hi, lo = _split2(log_1m)
    incl = _dot(hi, uincl) + _dot(lo, uincl)
    tot = jnp.broadcast_to(jnp.sum(log_1m, axis=-1, keepdims=True), (r, page))
    ri = lax.broadcasted_iota(jnp.int32, (r, r), 0)
    ci = lax.broadcasted_iota(jnp.int32, (r, r), 1)
    upage = jnp.where(((ri & 7) == (ci & 7)) & ((ci >> 3) > (ri >> 3)), 1.0, 0.0).astype(BF16)
    thi, tlo = _split2(tot)
    carry = _dot(upage, thi) + _dot(upage, tlo)
    a = jnp.exp(z - sp + (incl - log_1m) + carry).astype(BF16)
    osb = None
    for p in range(n_pages):
        d = _dot_nt(a[8 * p:8 * p + 8, :], vs[p][...].astype(BF16))
        osb = d if osb is None else osb + d
    res_s = jnp.sum(jnp.where((lane8 >> 6) == row8, osb, 0.0), axis=0, keepdims=True)
    os_ref[0] = _row_head_rms(res_s, gns_ref[...], 1.0)


def _dec_attn(page_flat, lam_params, gnd, gns, qd, kdn, vdn, qs, caches, layer, n_pages, lam_init):
    nb = qd.shape[0]
    page = caches[0].shape[3]
    assert page == 128
    small = lambda a: pl.BlockSpec(a.shape, lambda b, pt: (0, 0))
    rowspec = pl.BlockSpec((1, 1, ATT_W), lambda b, pt: (b, 0, 0))
    page_specs = []
    for _ in caches:
        for p in range(n_pages):
            page_specs.append(pl.BlockSpec((None, None, ATT_W, page),
                                           lambda b, pt, p=p: (layer, pt[b * n_pages + p], 0, 0)))
    page_args = [c for c in caches for _ in range(n_pages)]
    grid_spec = pltpu.PrefetchScalarGridSpec(
        num_scalar_prefetch=1, grid=(nb,),
        in_specs=[small(a) for a in lam_params] + [small(gnd), small(gns), rowspec, rowspec, rowspec, rowspec] + page_specs,
        out_specs=[rowspec, rowspec])
    return pl.pallas_call(
        functools.partial(_dec_attn_body, n_pages=n_pages, page=page, lam_init=lam_init),
        grid_spec=grid_spec,
        out_shape=[jax.ShapeDtypeStruct((nb, 1, ATT_W), F32)] * 2,
        compiler_params=_cparams("parallel"), name="dec_attn")(
            page_flat, *lam_params, gnd, gns, qd, kdn, vdn, qs, *page_args)


def _split_cols(w):
    outs, i = [], 0
    for n in IN_SIZES:
        outs.append(w[:, i:i + n])
        i += n
    return outs


def _pack_w_in(w):
    dq, dk, dv, z, xbc, dt, sq, sk, sv = _split_cols(w)
    dt = jnp.pad(dt, ((0, 0), (0, DT_PAD - dt.shape[1])))
    w_nat = jnp.concatenate([dq, z, xbc, sq, dt, dv, sv, dk], axis=1).astype(BF16)
    w_t = jnp.concatenate([dk, dv, sk, sv], axis=1).T.astype(BF16)
    return w_nat, w_t


def kernel(x_prompt, x_sample, cache_diff_k, cache_diff_v, cache_sb_k, cache_sb_v, state_ssm, state_conv, page_table, norm_ffn1, w_ffn1_gu, w_ffn1_down, norm_mix, w_in, diff_lambda_q1, diff_lambda_k1, diff_lambda_q2, diff_lambda_k2, diff_norm, ssd_conv_w, ssd_conv_b, ssd_dt_bias, ssd_a_log, ssd_d, ssd_norm, sb_norm, w_out, norm_ffn2, w_ffn2_gu, w_ffn2_down, norm_final):
    batch, seq, _ = x_prompt.shape
    nb = x_sample.shape[0]
    depth = w_in.shape[0]
    n_pages = page_table.shape[1]
    n_pool, page = cache_diff_k.shape[1], cache_diff_k.shape[2]
    tp = batch * seq
    xp = x_prompt.reshape(tp, D_MODEL)
    xs = x_sample.reshape(nb, D_MODEL)
    caches = [jnp.transpose(c, (0, 1, 3, 4, 2)).reshape(depth, n_pool, ATT_W, page)
              for c in (cache_diff_k, cache_diff_v, cache_sb_k, cache_sb_v)]
    conv_in = jnp.transpose(state_conv, (0, 2, 1, 3))
    page_flat = page_table.reshape(-1).astype(jnp.int32)
    row = lambda a: a.reshape(1, -1).astype(F32)
    gfinal = row(norm_final)
    new_p, new_s = [], []
    for l in range(depth):
        lam_init = 0.8 - 0.6 * math.exp(-0.3 * l)
        last = l == depth - 1
        wg1, wu1 = w_ffn1_gu[l][:, :D_FF].astype(BF16), w_ffn1_gu[l][:, D_FF:].astype(BF16)
        wd1 = w_ffn1_down[l].astype(BF16)
        wg2, wu2 = w_ffn2_gu[l][:, :D_FF].astype(BF16), w_ffn2_gu[l][:, D_FF:].astype(BF16)
        wd2 = w_ffn2_down[l].astype(BF16)
        w_nat, w_t = _pack_w_in(w_in[l])
        w_out_b = w_out[l].astype(BF16)
        lam_params = [row(a[l]) for a in (diff_lambda_q1, diff_lambda_k1, diff_lambda_q2, diff_lambda_k2)]
        gnd = row(jnp.tile(diff_norm[l], 4))
        gns = row(jnp.tile(sb_norm[l], 4))
        cw, cb = ssd_conv_w[l].astype(F32), row(ssd_conv_b[l])
        dtb = row(jnp.pad(ssd_dt_bias[l], (0, DT_PAD - SSD_HEADS)))
        alog = row(jnp.pad(ssd_a_log[l], (0, DT_PAD - SSD_HEADS)))
        dfull = row(jnp.repeat(ssd_d[l], HEAD_W))
        nw = row(ssd_norm[l])
        g1, gm, g2 = row(norm_ffn1[l]), row(norm_mix[l]), row(norm_ffn2[l])

        xp = _ffn(xp, g1, wg1, wu1, wd1)
        (dq, z, xbc, sq, dt, dk_t, dv_t, sk_t, sv_t, dk_a, sk_a, dv_a, sv_a) = _inproj(
            xp, gm, w_nat, w_t, batch, seq, sample=False)
        od = _diff_attn(dq, dk_a, dv_a, lam_params, gnd, batch, seq, lam_init)
        os_ = _sb_attn(sq, sk_a, sv_a, gns, batch, seq)
        y, h_p, conv_p = _ssd_prompt(z, xbc, dt, cw, cb, dtb, alog, dfull, nw, batch, seq)
        xp = _outproj(xp, od, y, os_, w_out_b)
        xp = _ffn(xp, g2, wg2, wu2, wd2, gfinal if last else None)
        new_p.append((dk_t, dv_t, sk_t, sv_t, h_p, conv_p))

        xs = _ffn(xs, g1, wg1, wu1, wd1)
        (dq, z, xbc, sq, dt, dk_t, dv_t, sk_t, sv_t, dk_n, dv_n) = _inproj(
            xs, gm, w_nat, w_t, 1, nb, sample=True)
        r3 = lambda a: a.reshape(nb, 1, ATT_W)
        od, os_ = _dec_attn(page_flat, lam_params, gnd, gns, r3(dq), r3(dk_n), r3(dv_n), r3(sq), caches, l,
                            n_pages, lam_init)
        y, h_s, conv_s = _ssd_sample(z, xbc, dt, conv_in, state_ssm, l, cw, cb, dtb, alog, dfull, nw)
        xs = _outproj(xs, od.reshape(nb, ATT_W), y, os_.reshape(nb, ATT_W), w_out_b)
        xs = _ffn(xs, g2, wg2, wu2, wd2, gfinal if last else None)
        new_s.append((dk_t, dv_t, sk_t, sv_t, h_s, conv_s))

    stk = lambda states, i: jnp.stack([st[i] for st in states], axis=0)
    kv_p = lambda i: jnp.transpose(stk(new_p, i).reshape(depth, batch, 4, HEAD_W, seq), (0, 1, 4, 2, 3))
    kv_s = lambda i: jnp.transpose(stk(new_s, i).reshape(depth, 4, HEAD_W, nb), (0, 3, 1, 2)).reshape(
        depth, nb, 1, 4, HEAD_W)
    conv_s_out = jnp.transpose(stk(new_s, 5), (0, 2, 1, 3))
    return (xp.reshape(batch, seq, D_MODEL), xs.reshape(nb, 1, D_MODEL),
            kv_p(0), kv_p(1), kv_p(2), kv_p(3), stk(new_p, 4), stk(new_p, 5),
            kv_s(0), kv_s(1), kv_s(2), kv_s(3), stk(new_s, 4), conv_s_out)
```

```python
import functools
import math

import jax
import jax.numpy as jnp
from jax import lax
from jax.experimental import pallas as pl
from jax.experimental.pallas import tpu as pltpu

F32, BF16 = jnp.float32, jnp.bfloat16
HIGHEST = lax.Precision.HIGHEST

D_MODEL = 1024
D_FF = 2816
DIFF_QK = 32
HEAD_W = 64
ATT_W = 256
SSD_W = 512
SSD_HEADS = 8
SSD_STATE = 128
SSD_CHUNK = 128
CONV_K = 4
CONV_CH = 1024
EPS = 1e-6
NEG = -1e30
IN_SIZES = (256, 256, 256, 512, 1024, 8, 256, 256, 256)
DT_PAD = 128
FF_CHUNK = 256
VMEM_LIMIT = 56 * 1024 * 1024


def _cparams(*sem):
    return pltpu.CompilerParams(dimension_semantics=sem, vmem_limit_bytes=VMEM_LIMIT)


def _rms(x, g):
    return x * lax.rsqrt(jnp.mean(x * x, axis=-1, keepdims=True) + EPS) * g


def _softplus(x):
    return jnp.maximum(x, 0.0) + jnp.log1p(jnp.exp(-jnp.abs(x)))


LOG2E = 1.4426950408889634


def _softplus2(x):
    return jnp.maximum(x, 0.0) + jnp.log2(1.0 + jnp.exp2(-jnp.abs(x)))


def _dot(a, b):
    return jnp.dot(a, b, preferred_element_type=F32)


def _dot_nt(a, b):
    return lax.dot_general(a, b, (((1,), (1,)), ((), ())), preferred_element_type=F32)


def _dot_f32(a, b):
    return jnp.dot(a, b, precision=HIGHEST, preferred_element_type=F32)


def _split2(x):
    hi = x.astype(BF16)
    lo = (x - hi.astype(F32)).astype(BF16)
    return hi, lo


def _const_spec(shape):
    nd = len(shape)
    return pl.BlockSpec(shape, lambda *_: (0,) * nd, pipeline_mode=pl.Buffered(1))


def _ffn_body(*refs, n_chunks, final, mix):
    refs = list(refs)
    x_ref, g_ref, wg_ref, wu_ref, wd_ref = refs[:5]
    o_ref = refs.pop()
    gf_ref = refs.pop() if final else None
    x = x_ref[...]
    if mix:
        od_ref, y_ref, os_ref, wo_ref = refs[5:9]
        x = x + _dot(od_ref[...].astype(BF16), wo_ref[0:ATT_W, :])
        x = x + _dot(y_ref[...].astype(BF16), wo_ref[ATT_W:ATT_W + SSD_W, :])
        x = x + _dot(os_ref[...].astype(BF16), wo_ref[ATT_W + SSD_W:, :])
    xn = _rms(x, g_ref[...]).astype(BF16)
    acc = jnp.zeros_like(x)
    for c in range(n_chunks):
        sl = slice(c * FF_CHUNK, (c + 1) * FF_CHUNK)
        gate = _dot(xn, wg_ref[:, sl])
        up = _dot(xn, wu_ref[:, sl])
        h = (gate * jax.nn.sigmoid(gate) * up).astype(BF16)
        acc = acc + _dot(h, wd_ref[sl, :])
    y = x + 0.5 * acc
    if final:
        y = _rms(y, gf_ref[...])
    o_ref[...] = y


def _ffn(x, g, wg, wu, wd, final_g=None, mix=None):
    t = x.shape[0]
    tm = min(512, t)
    assert t % tm == 0 and D_FF % FF_CHUNK == 0
    rows = lambda w: pl.BlockSpec((tm, w), lambda i: (i, 0))
    row = rows(D_MODEL)
    in_specs = [row, _const_spec((1, D_MODEL)), _const_spec((D_MODEL, D_FF)), _const_spec((D_MODEL, D_FF)),
                _const_spec((D_FF, D_MODEL))]
    args = [x, g, wg, wu, wd]
    if mix is not None:
        in_specs += [rows(ATT_W), rows(SSD_W), rows(ATT_W), _const_spec((D_MODEL, D_MODEL))]
        args += list(mix)
    if final_g is not None:
        in_specs.append(_const_spec((1, D_MODEL)))
        args.append(final_g)
    return pl.pallas_call(
        functools.partial(_ffn_body, n_chunks=D_FF // FF_CHUNK, final=final_g is not None, mix=mix is not None),
        grid=(t // tm,), in_specs=in_specs, out_specs=row,
        out_shape=jax.ShapeDtypeStruct((t, D_MODEL), F32),
        compiler_params=_cparams("parallel"), name="ffn")(*args)


_NAT_W = (512, 1024, DT_PAD, 256, 256, 256, 256, 256)
_NAT_Z, _NAT_XBC, _NAT_DT, _NAT_DK, _NAT_SK, _NAT_DQ, _NAT_SQ, _NAT_DV = range(8)
_T_DK, _T_DV, _T_SK, _T_SV, _T_DQ, _T_SQ = range(6)
ATT_TK = 256


def _inproj_body(x_ref, g_ref, wn_ref, wt_ref, *o_refs, sample, tk):
    u = _rms(x_ref[...], g_ref[...]).astype(BF16)
    offs = [sum(_NAT_W[:i]) for i in range(len(_NAT_W))]
    nat = lambda i: _dot(u, wn_ref[:, offs[i]:offs[i] + _NAT_W[i]])
    tr = lambda j: _dot_nt(wt_ref[j * ATT_W:(j + 1) * ATT_W, :], u)
    o = list(o_refs)
    for i in (_NAT_Z, _NAT_XBC, _NAT_DT):
        o.pop(0)[...] = nat(i)
    kv_t = [tr(j) for j in (_T_DK, _T_DV, _T_SK, _T_SV)]
    for v in kv_t:
        o.pop(0)[...] = v
    if sample:
        for i in (_NAT_DQ, _NAT_SQ, _NAT_DK, _NAT_DV):
            o.pop(0)[...] = nat(i)
    else:
        o.pop(0)[...] = tr(_T_DQ)
        o.pop(0)[...] = tr(_T_SQ)
        o.pop(0)[...] = nat(_NAT_DK).astype(BF16)
        o.pop(0)[...] = nat(_NAT_SK).astype(BF16)
        for v in (kv_t[1], kv_t[3]):
            r = o.pop(0)
            for c in range(v.shape[1] // tk):
                r[c] = v[:, c * tk:(c + 1) * tk].astype(BF16)
    assert not o


def _inproj(x, g, w_nat, w_t, batch, seq, sample):
    t = x.shape[0]
    tm = min(512, seq)
    tk = min(ATT_TK, seq)
    assert seq % tm == 0 and tm % tk == 0
    nj = seq // tm
    rows = lambda w: pl.BlockSpec((tm, w), lambda i: (i, 0))
    tspec = pl.BlockSpec((None, ATT_W, tm), lambda i: (i // nj, 0, i % nj))
    tshape = jax.ShapeDtypeStruct((batch, ATT_W, seq), F32)
    out_specs = [rows(_NAT_W[i]) for i in (_NAT_Z, _NAT_XBC, _NAT_DT)] + [tspec] * 4
    out_shape = [jax.ShapeDtypeStruct((t, _NAT_W[i]), F32) for i in (_NAT_Z, _NAT_XBC, _NAT_DT)] + [tshape] * 4
    if sample:
        out_specs += [rows(ATT_W)] * 4
        out_shape += [jax.ShapeDtypeStruct((t, ATT_W), F32)] * 4
    else:
        out_specs += [tspec] * 2 + [rows(ATT_W)] * 2
        out_shape += [tshape] * 2 + [jax.ShapeDtypeStruct((t, ATT_W), BF16)] * 2
        out_specs += [pl.BlockSpec((None, tm // tk, ATT_W, tk), lambda i: (i // nj, i % nj, 0, 0))] * 2
        out_shape += [jax.ShapeDtypeStruct((batch, seq // tk, ATT_W, tk), BF16)] * 2
    return pl.pallas_call(
        functools.partial(_inproj_body, sample=sample, tk=tk), grid=(t // tm,),
        in_specs=[rows(D_MODEL), _const_spec((1, D_MODEL)), _const_spec(w_nat.shape), _const_spec(w_t.shape)],
        out_specs=out_specs, out_shape=out_shape,
        compiler_params=_cparams("parallel"), name="inproj")(x, g, w_nat, w_t)


def _lam(lq1, lk1, lq2, lk2, lam_init):
    s1 = jnp.sum(lq1[...] * lk1[...], axis=-1, keepdims=True)
    s2 = jnp.sum(lq2[...] * lk2[...], axis=-1, keepdims=True)
    return jnp.exp(s1) - jnp.exp(s2) + lam_init


def _attn_specs(gain_args, batch, seq, tq):
    nq = seq // tq
    small = lambda a: pl.BlockSpec(a.shape, lambda b, i: (0, 0))
    in_specs = [small(a) for a in gain_args] + [
        pl.BlockSpec((None, ATT_W, tq), lambda b, i: (b, 0, i)),
        pl.BlockSpec((seq, ATT_W), lambda b, i: (b, 0)),
        pl.BlockSpec((None, nq, ATT_W, tq), lambda b, i: (b, 0, 0, 0))]
    out_spec = pl.BlockSpec((tq, ATT_W), lambda b, i: (b * nq + i, 0))
    return in_specs, out_spec


def _norm_heads_t(o_t, gcol_ref, scale):
    parts = []
    for h in range(4):
        rows = slice(h * HEAD_W, (h + 1) * HEAD_W)
        oh = o_t[rows, :]
        ms = jnp.mean(oh * oh, axis=0, keepdims=True)
        parts.append(oh * lax.rsqrt(ms + EPS) * (gcol_ref[rows, :] * scale))
    return jnp.concatenate(parts, axis=0).T


def _diff_attn_body(lq1, lk1, lq2, lk2, gcol_ref, q_ref, k_ref, v_ref, o_ref,
                    qs_ref, m_ref, l_ref, acc_ref, *, tq, lam_init):
    qi = pl.program_id(1)
    q = q_ref[...] * (DIFF_QK ** -0.5 * LOG2E)
    seg = lax.broadcasted_iota(jnp.int32, (ATT_W, tq), 0) >> 5
    for j in range(8):
        qs_ref[j] = jnp.where(seg == j, q, 0.0).astype(BF16)
    m_ref[...] = jnp.full((8, tq), NEG, F32)
    l_ref[...] = jnp.zeros((8, tq), F32)
    acc_ref[...] = jnp.zeros((2, ATT_W, tq), F32)

    def step(kj, masked):
        start = pl.multiple_of(kj * tq, tq)
        kblk = k_ref[pl.ds(start, tq), :]
        if masked:
            keep = (lax.broadcasted_iota(jnp.int32, (tq, tq), 0) <= lax.broadcasted_iota(jnp.int32, (tq, tq), 1))
        scores = [_dot(kblk, qs_ref[j]) for j in range(8)]
        m_all, l_all = m_ref[...], l_ref[...]
        m_rows, l_rows = [], []
        for j in range(8):
            h, mi = j // 2, j % 2
            rows = slice(h * HEAD_W, (h + 1) * HEAD_W)
            s = scores[j]
            if masked:
                s = jnp.where(keep, s, NEG)
            m_prev = m_all[j:j + 1, :]
            m_new = jnp.maximum(m_prev, jnp.max(s, axis=0, keepdims=True))
            alpha = jnp.exp2(m_prev - m_new)
            p = jnp.exp2(s - m_new)
            l_rows.append(alpha * l_all[j:j + 1, :] + jnp.sum(p, axis=0, keepdims=True))
            m_rows.append(m_new)
            acc_ref[mi, rows, :] = acc_ref[mi, rows, :] * alpha + _dot(v_ref[kj, rows, :], p.astype(BF16))
        m_ref[...] = jnp.concatenate(m_rows, axis=0)
        l_ref[...] = jnp.concatenate(l_rows, axis=0)

    def loop_body(kj, carry):
        step(kj, False)
        return carry

    lax.fori_loop(0, qi, loop_body, 0)
    step(qi, True)

    lam = _lam(lq1, lk1, lq2, lk2, lam_init)
    linv = 1.0 / l_ref[...]
    parts = []
    for h in range(4):
        rows = slice(h * HEAD_W, (h + 1) * HEAD_W)
        parts.append(acc_ref[0, rows, :] * linv[2 * h:2 * h + 1, :]
                     - lam * (acc_ref[1, rows, :] * linv[2 * h + 1:2 * h + 2, :]))
    o_ref[...] = _norm_heads_t(jnp.concatenate(parts, axis=0), gcol_ref, 1.0 - lam_init)


def _diff_attn(q_t, k, v_t, lam_params, gcol, batch, seq, lam_init):
    tq = min(ATT_TK, seq)
    assert seq % tq == 0
    in_specs, out_spec = _attn_specs(list(lam_params) + [gcol], batch, seq, tq)
    return pl.pallas_call(
        functools.partial(_diff_attn_body, tq=tq, lam_init=lam_init),
        grid=(batch, seq // tq), in_specs=in_specs, out_specs=out_spec,
        out_shape=jax.ShapeDtypeStruct((batch * seq, ATT_W), F32),
        scratch_shapes=[pltpu.VMEM((8, ATT_W, tq), BF16), pltpu.VMEM((8, tq), F32),
                        pltpu.VMEM((8, tq), F32), pltpu.VMEM((2, ATT_W, tq), F32)],
        compiler_params=_cparams("parallel", "arbitrary"), name="diff_attn")(*lam_params, gcol, q_t, k, v_t)


def _sb_attn_body(gcol_ref, q_ref, k_ref, v_ref, o_ref, qs_ref, c_ref, acc_ref, *, tq):
    qi = pl.program_id(1)
    q = q_ref[...] * (HEAD_W ** -0.5 * LOG2E)
    hrow = lax.broadcasted_iota(jnp.int32, (ATT_W, tq), 0) >> 6
    for h in range(4):
        qs_ref[h] = jnp.where(hrow == h, q, 0.0).astype(BF16)
    c_ref[...] = jnp.zeros((8, tq), F32)
    acc_ref[...] = jnp.zeros((ATT_W, tq), F32)
    ki = lax.broadcasted_iota(jnp.int32, (tq, tq), 0)
    kl = lax.broadcasted_iota(jnp.int32, (tq, tq), 1)
    later = jnp.where(kl >= ki, 1.0, 0.0).astype(BF16)
    later2 = jnp.concatenate([later, later], axis=1)

    def step(kj, masked):
        start = pl.multiple_of(kj * tq, tq)
        kblk = k_ref[pl.ds(start, tq), :]
        if masked:
            keep = ki < kl
        zs = [_dot(kblk, qs_ref[h]) for h in range(4)]
        sps, incls = [], []
        for h in range(4):
            sp = _softplus2(zs[h])
            if masked:
                sp = jnp.where(keep, sp, 0.0)
            hi, lo = _split2(sp)
            incls.append(_dot(later2, jnp.concatenate([hi, lo], axis=0)))
            sps.append(sp)
        c_all = c_ref[...]
        c_rows = []
        for h in range(4):
            rows = slice(h * HEAD_W, (h + 1) * HEAD_W)
            c = c_all[h:h + 1, :]
            a = jnp.exp2(zs[h] - (c + incls[h]))
            if masked:
                a = jnp.where(keep, a, 0.0)
            c_rows.append(c + jnp.sum(sps[h], axis=0, keepdims=True))
            acc_ref[rows, :] = acc_ref[rows, :] + _dot(v_ref[kj, rows, :], a.astype(BF16))
        c_ref[...] = jnp.concatenate(c_rows + [c_all[4:8, :]], axis=0)

    step(qi, True)

    def loop_body(i, carry):
        step(qi - 1 - i, False)
        return carry

    lax.fori_loop(0, qi, loop_body, 0)
    o_ref[...] = _norm_heads_t(acc_ref[...], gcol_ref, 1.0)


def _sb_attn(q_t, k, v_t, gcol, batch, seq):
    tq = min(ATT_TK, seq)
    assert seq % tq == 0
    in_specs, out_spec = _attn_specs([gcol], batch, seq, tq)
    return pl.pallas_call(
        functools.partial(_sb_attn_body, tq=tq),
        grid=(batch, seq // tq), in_specs=in_specs, out_specs=out_spec,
        out_shape=jax.ShapeDtypeStruct((batch * seq, ATT_W), F32),
        scratch_shapes=[pltpu.VMEM((4, ATT_W, tq), BF16), pltpu.VMEM((8, tq), F32),
                        pltpu.VMEM((ATT_W, tq), F32)],
        compiler_params=_cparams("parallel", "arbitrary"), name="sb_attn")(gcol, q_t, k, v_t)


def _expand_mat():
    r = lax.broadcasted_iota(jnp.int32, (DT_PAD, SSD_W), 0)
    c = lax.broadcasted_iota(jnp.int32, (DT_PAD, SSD_W), 1) >> 6
    return jnp.where(r == c, 1.0, 0.0).astype(F32)


def _a_row(alog_ref):
    lane = lax.broadcasted_iota(jnp.int32, (1, DT_PAD), 1)
    return jnp.where(lane < SSD_HEADS, -jnp.exp(alog_ref[...]), 0.0)


def _gate_norm(y, z, nw):
    y = y * (z * jax.nn.sigmoid(z))
    half = SSD_W // 2
    parts = [_rms(y[:, g * half:(g + 1) * half], nw[:, g * half:(g + 1) * half]) for g in range(2)]
    return jnp.concatenate(parts, axis=1)


def _ssd_body(z_ref, xbc_ref, dt_ref, cw_ref, cb_ref, dtb_ref, alog_ref, dfull_ref, nw_ref,
              y_ref, hout_ref, cout_ref, ext_ref, ht_ref):
    c = pl.program_id(1)
    q = SSD_CHUNK

    @pl.when(c == 0)
    def _():
        ext_ref[0:8, :] = jnp.zeros((8, CONV_CH), F32)
        ht_ref[...] = jnp.zeros_like(ht_ref)

    ext_ref[8:8 + q, :] = xbc_ref[...]
    conv = cb_ref[...]
    for k in range(CONV_K):
        conv = conv + cw_ref[k:k + 1, :] * ext_ref[5 + k:5 + k + q, :]
    tail = ext_ref[q + 5:q + 8, :]
    ext_ref[5:8, :] = tail
    xc = conv * jax.nn.sigmoid(conv)
    xs = xc[:, :SSD_W]
    dt = _softplus(dt_ref[...] + dtb_ref[...])
    dta = dt * _a_row(alog_ref)
    ti = lax.broadcasted_iota(jnp.int32, (q, q), 0)
    si = lax.broadcasted_iota(jnp.int32, (q, q), 1)
    causal = ti >= si
    cum = _dot_f32(jnp.where(causal, 1.0, 0.0).astype(F32), dta)
    emat = _expand_mat()
    cum_full = _dot_f32(cum, emat)
    xdt = xs * _dot_f32(dt, emat)
    cum_last = cum_full[q - 1:q, :]
    xdec = xdt * jnp.exp(cum_last - cum_full)
    exp_cum = jnp.exp(cum_full)
    cum_t = cum.T
    lane = lax.broadcasted_iota(jnp.int32, (1, 128), 1)
    y_slabs = []
    for g in range(2):
        bm = xc[:, SSD_W + g * SSD_STATE:SSD_W + (g + 1) * SSD_STATE]
        cm = xc[:, SSD_W + 2 * SSD_STATE + g * SSD_STATE:SSD_W + 2 * SSD_STATE + (g + 1) * SSD_STATE]
        bmb, cmb = bm.astype(BF16), cm.astype(BF16)
        cb = _dot_nt(cmb, bmb)
        bm_t = bm.T.astype(BF16)
        for j in (2 * g, 2 * g + 1):
            sl = slice(j * 128, (j + 1) * 128)
            xdt_s = xdt[:, sl]
            y_in = None
            for hh in range(2):
                h = 2 * j + hh
                segd = cum[:, h:h + 1] - cum_t[h:h + 1, :]
                w = (cb * jnp.exp(jnp.where(causal, segd, NEG))).astype(BF16)
                xh = jnp.where((lane >> 6) == hh, xdt_s, 0.0).astype(BF16)
                d = _dot(w, xh)
                y_in = d if y_in is None else y_in + d
            ht = ht_ref[j]
            y_x = _dot(cmb, ht.astype(BF16)) * exp_cum[:, sl]
            st = _dot(bm_t, xdec[:, sl].astype(BF16))
            ht_ref[j] = jnp.exp(cum_last[:, sl]) * ht + st
            y_slabs.append(y_in + y_x + dfull_ref[:, sl] * xs[:, sl])
    y = jnp.concatenate(y_slabs, axis=1)
    y_ref[...] = _gate_norm(y, z_ref[...], nw_ref[...])

    @pl.when(c == pl.num_programs(1) - 1)
    def _():
        for j in range(4):
            hout_ref[0, 2 * j:2 * j + 2] = ht_ref[j].T.reshape(2, HEAD_W, SSD_STATE)
        cout_ref[0] = tail


def _ssd_prompt(z, xbc, dt, cw, cb, dtb, alog, dfull, nw, batch, seq):
    q = SSD_CHUNK
    nc = seq // q
    rowspec = lambda w: pl.BlockSpec((q, w), lambda b, c: (b * nc + c, 0))
    small = lambda a: pl.BlockSpec(a.shape, lambda b, c: (0, 0))
    return pl.pallas_call(
        _ssd_body, grid=(batch, nc),
        in_specs=[rowspec(SSD_W), rowspec(CONV_CH), rowspec(DT_PAD)] + [small(a) for a in (cw, cb, dtb, alog, dfull, nw)],
        out_specs=[rowspec(SSD_W),
                   pl.BlockSpec((1, SSD_HEADS, HEAD_W, SSD_STATE), lambda b, c: (b, 0, 0, 0)),
                   pl.BlockSpec((1, CONV_K - 1, CONV_CH), lambda b, c: (b, 0, 0))],
        out_shape=[jax.ShapeDtypeStruct((batch * seq, SSD_W), F32),
                   jax.ShapeDtypeStruct((batch, SSD_HEADS, HEAD_W, SSD_STATE), F32),
                   jax.ShapeDtypeStruct((batch, CONV_K - 1, CONV_CH), F32)],
        scratch_shapes=[pltpu.VMEM((q + 8, CONV_CH), F32), pltpu.VMEM((4, SSD_STATE, 128), F32)],
        compiler_params=_cparams("parallel", "arbitrary"), name="ssd_prompt")(z, xbc, dt, cw, cb, dtb, alog, dfull, nw)


_SB = 8


def _dec_ssd_body(z_ref, xbc_ref, dt_ref, cs_ref, h_ref, cw_ref, cb_ref, dtb_ref, alog_ref, dfull_ref, nw_ref,
                  y_ref, hout_ref, cout_ref, yrow_ref):
    x_new = xbc_ref[...]
    c0, c1, c2 = cs_ref[0], cs_ref[1], cs_ref[2]
    conv = cb_ref[...] + cw_ref[0:1, :] * c0 + cw_ref[1:2, :] * c1 + cw_ref[2:3, :] * c2 + cw_ref[3:4, :] * x_new
    cout_ref[0] = c1
    cout_ref[1] = c2
    cout_ref[2] = x_new
    xc = conv * jax.nn.sigmoid(conv)
    xs = xc[:, :SSD_W]
    bmat = xc[:, SSD_W:SSD_W + 2 * SSD_STATE]
    cmat = xc[:, SSD_W + 2 * SSD_STATE:]
    dt = _softplus(dt_ref[...] + dtb_ref[...])
    da = jnp.exp(dt * _a_row(alog_ref))
    emat = _expand_mat()
    xdt = xs * _dot_f32(dt, emat)
    da_full = _dot_f32(da, emat)
    x_hi, x_lo = _split2(xdt)
    d_hi = da_full.astype(BF16)
    d_r = da_full - d_hi.astype(F32)
    d_mid = d_r.astype(BF16)
    d_lo = (d_r - d_mid.astype(F32)).astype(BF16)
    g0 = lax.broadcasted_iota(jnp.int32, (_SB, SSD_W), 1) < SSD_W // 2
    zx = jnp.zeros((_SB, SSD_W), F32)
    f = lambda a: a.astype(F32)
    x0h, x0l = jnp.where(g0, f(x_hi), 0.0), jnp.where(g0, f(x_lo), 0.0)
    x1h, x1l = jnp.where(g0, 0.0, f(x_hi)), jnp.where(g0, 0.0, f(x_lo))
    xrows = [x0h, x0h, x0l, x1h, x1h, x1l, f(d_hi), f(d_mid), f(d_lo)] + [zx] * 7
    xt = jnp.concatenate(xrows, axis=0).T.astype(BF16)
    b_hi, b_lo = (f(a) for a in _split2(bmat))
    zb = jnp.zeros((_SB, SSD_STATE), F32)
    ob = jnp.ones((_SB, SSD_STATE), F32)
    cat = lambda a, b: jnp.concatenate([a, b], axis=1)
    b0h, b0l, b1h, b1l = b_hi[:, :128], b_lo[:, :128], b_hi[:, 128:], b_lo[:, 128:]
    rrows = [cat(b0h, zb), cat(b0l, zb), cat(b0h, zb), cat(b1h, zb), cat(b1l, zb), cat(b1h, zb),
             cat(zb, ob), cat(zb, ob), cat(zb, ob)] + [cat(zb, zb)] * 7
    r_all = jnp.concatenate(rrows, axis=0)
    rowi = lax.broadcasted_iota(jnp.int32, (16 * _SB, 2 * SSD_STATE), 0) & (_SB - 1)
    row8 = lax.broadcasted_iota(jnp.int32, (8, SSD_STATE), 0)
    lane512 = lax.broadcasted_iota(jnp.int32, (1, SSD_W), 1)
    for i in range(_SB):
        out = _dot(xt, jnp.where(rowi == i, r_all, 0.0).astype(BF16))
        h_new = out[:, SSD_STATE:] * h_ref[i].reshape(SSD_W, SSD_STATE) + out[:, :SSD_STATE]
        hout_ref[i] = h_new.reshape(SSD_HEADS, HEAD_W, SSD_STATE)
        crow = jnp.where(row8 == 0, cmat[i:i + 1, :SSD_STATE],
                         jnp.where(row8 == 1, cmat[i:i + 1, SSD_STATE:], 0.0))
        yy = _dot_nt(crow.astype(BF16), h_new.astype(BF16))
        yrow_ref[i:i + 1, :] = jnp.where(lane512 < SSD_W // 2, yy[0:1, :], yy[1:2, :])
    y = yrow_ref[...] + dfull_ref[...] * xs
    y_ref[...] = _gate_norm(y, z_ref[...], nw_ref[...])


def _ssd_sample(z, xbc, dt, conv_state, ssm_state, layer, cw, cb, dtb, alog, dfull, nw):
    nb = z.shape[0]
    assert nb % _SB == 0
    rowspec = lambda w: pl.BlockSpec((_SB, w), lambda i: (i, 0))
    small = lambda a: pl.BlockSpec(a.shape, lambda i: (0, 0))
    cshape, hshape = (CONV_K - 1, _SB, CONV_CH), (_SB, SSD_HEADS, HEAD_W, SSD_STATE)
    cspec = pl.BlockSpec(cshape, lambda i: (0, i, 0))
    hspec = pl.BlockSpec(hshape, lambda i: (i, 0, 0, 0))
    cin = pl.BlockSpec((None,) + cshape, lambda i: (layer, 0, i, 0))
    hin = pl.BlockSpec((None,) + hshape, lambda i: (layer, i, 0, 0, 0))
    return pl.pallas_call(
        _dec_ssd_body, grid=(nb // _SB,),
        in_specs=[rowspec(SSD_W), rowspec(CONV_CH), rowspec(DT_PAD), cin, hin] + [small(a) for a in (cw, cb, dtb, alog, dfull, nw)],
        out_specs=[rowspec(SSD_W), hspec, cspec],
        out_shape=[jax.ShapeDtypeStruct((nb, SSD_W), F32),
                   jax.ShapeDtypeStruct(ssm_state.shape[1:], F32),
                   jax.ShapeDtypeStruct(conv_state.shape[1:], F32)],
        scratch_shapes=[pltpu.VMEM((_SB, SSD_W), F32)],
        compiler_params=_cparams("parallel"), name="ssd_sample")(z, xbc, dt, conv_state, ssm_state, cw, cb, dtb, alog, dfull, nw)


def _row_head_rms(res, gain, scale):
    hl = lax.broadcasted_iota(jnp.int32, (1, ATT_W), 1) >> 6
    sq = res * res
    ms = jnp.zeros_like(res)
    for h in range(4):
        ms_h = jnp.sum(jnp.where(hl == h, sq, 0.0), axis=-1, keepdims=True) * (1.0 / HEAD_W)
        ms = jnp.where(hl == h, ms_h, ms)
    return res * lax.rsqrt(ms + EPS) * gain * scale


def _dec_attn_body(pt_ref, lq1, lk1, lq2, lk2, gnd_ref, gns_ref, qd_ref, kdn_ref, vdn_ref, qs_ref, *rest,
                   n_pages, page, lam_init):
    del pt_ref
    kd, vd = rest[0:n_pages], rest[n_pages:2 * n_pages]
    ks, vs = rest[2 * n_pages:3 * n_pages], rest[3 * n_pages:4 * n_pages]
    od_ref, os_ref = rest[4 * n_pages:]
    lane8 = lax.broadcasted_iota(jnp.int32, (8, ATT_W), 1)
    row8 = lax.broadcasted_iota(jnp.int32, (8, ATT_W), 0)

    qd = qd_ref[0] * (DIFF_QK ** -0.5)
    qm = jnp.where((lane8 >> 5) == row8, qd, 0.0)
    qmb = qm.astype(BF16)
    s_pages = [_dot(qmb, kd[p][...].astype(BF16)) for p in range(n_pages)]
    s_new = jnp.sum(qm * kdn_ref[0], axis=-1, keepdims=True)
    m = s_new
    for s in s_pages:
        m = jnp.maximum(m, jnp.max(s, axis=-1, keepdims=True))
    l = jnp.exp(s_new - m)
    o = l * vdn_ref[0]
    for p in range(n_pages):
        pr = jnp.exp(s_pages[p] - m)
        l = l + jnp.sum(pr, axis=-1, keepdims=True)
        o = o + _dot_nt(pr.astype(BF16), vd[p][...].astype(BF16))
    o = o / l
    lam = _lam(lq1, lk1, lq2, lk2, lam_init)
    coef = jnp.where((lane8 >> 6) == (row8 >> 1), jnp.where((row8 & 1) == 0, 1.0, -lam), 0.0)
    res = jnp.sum(coef * o, axis=0, keepdims=True)
    od_ref[0] = _row_head_rms(res, gnd_ref[...], 1.0 - lam_init)

    qs = qs_ref[0] * (HEAD_W ** -0.5)
    qsm = jnp.where((lane8 >> 6) == row8, qs, 0.0).astype(BF16)
    z = jnp.concatenate([_dot(qsm, ks[p][...].astype(BF16)) for p in range(n_pages)], axis=0)
    r = 8 * n_pages
    sp = _softplus(z)
    log_1m = -sp
    jj = lax.broadcasted_iota(jnp.int32, (page, page), 0)
    ss = lax.broadcasted_iota(jnp.int32, (page, page), 1)
    uincl = jnp.where(jj >= ss, 1.0, 0.0).astype(BF16)
    hi, lo = _split2(log_1m)
    incl = _dot(hi, uincl) + _dot(lo, uincl)
    tot = jnp.broadcast_to(jnp.sum(log_1m, axis=-1, keepdims=True), (r, page))
    ri = lax.broadcasted_iota(jnp.int32, (r, r), 0)
    ci = lax.broadcasted_iota(jnp.int32, (r, r), 1)
    upage = jnp.where(((ri & 7) == (ci & 7)) & ((ci >> 3) > (ri >> 3)), 1.0, 0.0).astype(BF16)
    thi, tlo = _split2(tot)
    carry = _dot(upage, thi) + _dot(upage, tlo)
    a = jnp.exp(z - sp + (incl - log_1m) + carry).astype(BF16)
    osb = None
    for p in range(n_pages):
        d = _dot_nt(a[8 * p:8 * p + 8, :], vs[p][...].astype(BF16))
        osb = d if osb is None else osb + d
    res_s = jnp.sum(jnp.where((lane8 >> 6) == row8, osb, 0.0), axis=0, keepdims=True)
    os_ref[0] = _row_head_rms(res_s, gns_ref[...], 1.0)


def _dec_attn(page_flat, lam_params, gnd, gns, qd, kdn, vdn, qs, caches, layer, n_pages, lam_init):
    nb = qd.shape[0]
    page = caches[0].shape[3]
    assert page == 128
    small = lambda a: pl.BlockSpec(a.shape, lambda b, pt: (0, 0))
    rowspec = pl.BlockSpec((1, 1, ATT_W), lambda b, pt: (b, 0, 0))
    page_specs = []
    for _ in caches:
        for p in range(n_pages):
            page_specs.append(pl.BlockSpec((None, None, ATT_W, page),
                                           lambda b, pt, p=p: (layer, pt[b * n_pages + p], 0, 0)))
    page_args = [c for c in caches for _ in range(n_pages)]
    grid_spec = pltpu.PrefetchScalarGridSpec(
        num_scalar_prefetch=1, grid=(nb,),
        in_specs=[small(a) for a in lam_params] + [small(gnd), small(gns), rowspec, rowspec, rowspec, rowspec] + page_specs,
        out_specs=[rowspec, rowspec])
    return pl.pallas_call(
        functools.partial(_dec_attn_body, n_pages=n_pages, page=page, lam_init=lam_init),
        grid_spec=grid_spec,
        out_shape=[jax.ShapeDtypeStruct((nb, 1, ATT_W), F32)] * 2,
        compiler_params=_cparams("parallel"), name="dec_attn")(
            page_flat, *lam_params, gnd, gns, qd, kdn, vdn, qs, *page_args)


def _split_cols(w):
    outs, i = [], 0
    for n in IN_SIZES:
        outs.append(w[:, i:i + n])
        i += n
    return outs


def _pack_w_in(w):
    dq, dk, dv, z, xbc, dt, sq, sk, sv = _split_cols(w)
    dt = jnp.pad(dt, ((0, 0), (0, DT_PAD - dt.shape[1])))
    w_nat = jnp.concatenate([z, xbc, dt, dk, sk, dq, sq, dv], axis=1).astype(BF16)
    w_t = jnp.concatenate([dk, dv, sk, sv, dq, sq], axis=1).T.astype(BF16)
    return w_nat, w_t


def kernel(x_prompt, x_sample, cache_diff_k, cache_diff_v, cache_sb_k, cache_sb_v, state_ssm, state_conv, page_table, norm_ffn1, w_ffn1_gu, w_ffn1_down, norm_mix, w_in, diff_lambda_q1, diff_lambda_k1, diff_lambda_q2, diff_lambda_k2, diff_norm, ssd_conv_w, ssd_conv_b, ssd_dt_bias, ssd_a_log, ssd_d, ssd_norm, sb_norm, w_out, norm_ffn2, w_ffn2_gu, w_ffn2_down, norm_final):
    batch, seq, _ = x_prompt.shape
    nb = x_sample.shape[0]
    depth = w_in.shape[0]
    n_pages = page_table.shape[1]
    n_pool, page = cache_diff_k.shape[1], cache_diff_k.shape[2]
    tp = batch * seq
    xp = x_prompt.reshape(tp, D_MODEL)
    xs = x_sample.reshape(nb, D_MODEL)
    caches = [jnp.transpose(c, (0, 1, 3, 4, 2)).reshape(depth, n_pool, ATT_W, page)
              for c in (cache_diff_k, cache_diff_v, cache_sb_k, cache_sb_v)]
    conv_in = jnp.transpose(state_conv, (0, 2, 1, 3))
    page_flat = page_table.reshape(-1).astype(jnp.int32)
    row = lambda a: a.reshape(1, -1).astype(F32)
    gfinal = row(norm_final)
    new_p, new_s = [], []
    for l in range(depth):
        lam_init = 0.8 - 0.6 * math.exp(-0.3 * l)
        last = l == depth - 1
        wg1, wu1 = w_ffn1_gu[l][:, :D_FF].astype(BF16), w_ffn1_gu[l][:, D_FF:].astype(BF16)
        wd1 = w_ffn1_down[l].astype(BF16)
        wg2, wu2 = w_ffn2_gu[l][:, :D_FF].astype(BF16), w_ffn2_gu[l][:, D_FF:].astype(BF16)
        wd2 = w_ffn2_down[l].astype(BF16)
        w_nat, w_t = _pack_w_in(w_in[l])
        w_out_b = w_out[l].astype(BF16)
        lam_params = [row(a[l]) for a in (diff_lambda_q1, diff_lambda_k1, diff_lambda_q2, diff_lambda_k2)]
        gnd = row(jnp.tile(diff_norm[l], 4))
        gns = row(jnp.tile(sb_norm[l], 4))
        cw, cb = ssd_conv_w[l].astype(F32), row(ssd_conv_b[l])
        dtb = row(jnp.pad(ssd_dt_bias[l], (0, DT_PAD - SSD_HEADS)))
        alog = row(jnp.pad(ssd_a_log[l], (0, DT_PAD - SSD_HEADS)))
        dfull = row(jnp.repeat(ssd_d[l], HEAD_W))
        nw = row(ssd_norm[l])
        g1, gm, g2 = row(norm_ffn1[l]), row(norm_mix[l]), row(norm_ffn2[l])

        xp = _ffn(xp, g1, wg1, wu1, wd1)
        (z, xbc, dt, dk_t, dv_t, sk_t, sv_t, dq_t, sq_t, dk_a, sk_a, dv_a, sv_a) = _inproj(
            xp, gm, w_nat, w_t, batch, seq, sample=False)
        od = _diff_attn(dq_t, dk_a, dv_a, lam_params, gnd.reshape(ATT_W, 1), batch, seq, lam_init)
        os_ = _sb_attn(sq_t, sk_a, sv_a, gns.reshape(ATT_W, 1), batch, seq)
        y, h_p, conv_p = _ssd_prompt(z, xbc, dt, cw, cb, dtb, alog, dfull, nw, batch, seq)
        xp = _ffn(xp, g2, wg2, wu2, wd2, gfinal if last else None, mix=(od, y, os_, w_out_b))
        new_p.append((dk_t, dv_t, sk_t, sv_t, h_p, conv_p))

        xs = _ffn(xs, g1, wg1, wu1, wd1)
        (z, xbc, dt, dk_t, dv_t, sk_t, sv_t, dq, sq, dk_n, dv_n) = _inproj(
            xs, gm, w_nat, w_t, 1, nb, sample=True)
        r3 = lambda a: a.reshape(nb, 1, ATT_W)
        od, os_ = _dec_attn(page_flat, lam_params, gnd, gns, r3(dq), r3(dk_n), r3(dv_n), r3(sq), caches, l,
                            n_pages, lam_init)
        y, h_s, conv_s = _ssd_sample(z, xbc, dt, conv_in, state_ssm, l, cw, cb, dtb, alog, dfull, nw)
        xs = _ffn(xs, g2, wg2, wu2, wd2, gfinal if last else None,
                  mix=(od.reshape(nb, ATT_W), y, os_.reshape(nb, ATT_W), w_out_b))
        new_s.append((dk_t, dv_t, sk_t, sv_t, h_s, conv_s))

    stk = lambda states, i: jnp.stack([st[i] for st in states], axis=0)
    kv_p = lambda i: jnp.transpose(stk(new_p, i).reshape(depth, batch, 4, HEAD_W, seq), (0, 1, 4, 2, 3))
    kv_s = lambda i: jnp.transpose(stk(new_s, i).reshape(depth, 4, HEAD_W, nb), (0, 3, 1, 2)).reshape(
        depth, nb, 1, 4, HEAD_W)
    conv_s_out = jnp.transpose(stk(new_s, 5), (0, 2, 1, 3))
    return (xp.reshape(batch, seq, D_MODEL), xs.reshape(nb, 1, D_MODEL),
            kv_p(0), kv_p(1), kv_p(2), kv_p(3), stk(new_p, 4), stk(new_p, 5),
            kv_s(0), kv_s(1), kv_s(2), kv_s(3), stk(new_s, 4), conv_s_out)
```

```python
import functools
import math

import jax
import jax.numpy as jnp
from jax import lax
from jax.experimental import pallas as pl
from jax.experimental.pallas import tpu as pltpu

F32, BF16 = jnp.float32, jnp.bfloat16

D_MODEL = 1024
D_FF = 2816
DIFF_QK = 32
HEAD_W = 64
ATT_W = 256
SSD_W = 512
SSD_HEADS = 8
SSD_STATE = 128
SSD_CHUNK = 128
CONV_K = 4
CONV_CH = 1024
EPS = 1e-6
NEG = -1e30
IN_SIZES = (256, 256, 256, 512, 1024, 8, 256, 256, 256)
DT_PAD = 128
FF_CHUNK = 256
VMEM_LIMIT = 56 * 1024 * 1024


def _cparams(*sem):
    return pltpu.CompilerParams(dimension_semantics=sem, vmem_limit_bytes=VMEM_LIMIT)


def _rms(x, g):
    return x * lax.rsqrt(jnp.mean(x * x, axis=-1, keepdims=True) + EPS) * g


def _softplus(x):
    return jnp.maximum(x, 0.0) + jnp.log1p(jnp.exp(-jnp.abs(x)))


LOG2E = 1.4426950408889634


def _softplus2(x):
    return jnp.maximum(x, 0.0) + jnp.log2(1.0 + jnp.exp2(-jnp.abs(x)))


def _dot(a, b):
    return jnp.dot(a, b, preferred_element_type=F32)


def _dot_nt(a, b):
    return lax.dot_general(a, b, (((1,), (1,)), ((), ())), preferred_element_type=F32)


def _split2(x):
    hi = x.astype(BF16)
    lo = (x - hi.astype(F32)).astype(BF16)
    return hi, lo


def _split3(x):
    hi = x.astype(BF16)
    r = x - hi.astype(F32)
    mid = r.astype(BF16)
    return hi, mid, (r - mid.astype(F32)).astype(BF16)


def _dot_sel_l(sel, x):
    return sum(_dot(sel, p) for p in _split3(x))


def _dot_sel_r(x, sel):
    return sum(_dot(p, sel) for p in _split3(x))


def _const_spec(shape):
    nd = len(shape)
    return pl.BlockSpec(shape, lambda *_: (0,) * nd, pipeline_mode=pl.Buffered(1))


def _ffn_body(*refs, n_chunks, final, mix):
    refs = list(refs)
    x_ref, g_ref, wg_ref, wu_ref, wd_ref = refs[:5]
    o_ref = refs.pop()
    gf_ref = refs.pop() if final else None
    x = x_ref[...]
    if mix:
        od_ref, y_ref, os_ref, wo_ref = refs[5:9]
        x = x + _dot(od_ref[...].astype(BF16), wo_ref[0:ATT_W, :])
        x = x + _dot(y_ref[...].astype(BF16), wo_ref[ATT_W:ATT_W + SSD_W, :])
        x = x + _dot(os_ref[...].astype(BF16), wo_ref[ATT_W + SSD_W:, :])
    xn = _rms(x, g_ref[...]).astype(BF16)
    acc = jnp.zeros_like(x)
    for c in range(n_chunks):
        sl = slice(c * FF_CHUNK, (c + 1) * FF_CHUNK)
        gate = _dot(xn, wg_ref[:, sl])
        up = _dot(xn, wu_ref[:, sl])
        h = (gate * jax.nn.sigmoid(gate) * up).astype(BF16)
        acc = acc + _dot(h, wd_ref[sl, :])
    y = x + 0.5 * acc
    if final:
        y = _rms(y, gf_ref[...])
    o_ref[...] = y


def _ffn(x, g, wg, wu, wd, final_g=None, mix=None):
    t = x.shape[0]
    tm = min(512, t)
    assert t % tm == 0 and D_FF % FF_CHUNK == 0
    rows = lambda w: pl.BlockSpec((tm, w), lambda i: (i, 0))
    row = rows(D_MODEL)
    in_specs = [row, _const_spec((1, D_MODEL)), _const_spec((D_MODEL, D_FF)), _const_spec((D_MODEL, D_FF)),
                _const_spec((D_FF, D_MODEL))]
    args = [x, g, wg, wu, wd]
    if mix is not None:
        in_specs += [rows(ATT_W), rows(SSD_W), rows(ATT_W), _const_spec((D_MODEL, D_MODEL))]
        args += list(mix)
    if final_g is not None:
        in_specs.append(_const_spec((1, D_MODEL)))
        args.append(final_g)
    return pl.pallas_call(
        functools.partial(_ffn_body, n_chunks=D_FF // FF_CHUNK, final=final_g is not None, mix=mix is not None),
        grid=(t // tm,), in_specs=in_specs, out_specs=row,
        out_shape=jax.ShapeDtypeStruct((t, D_MODEL), F32),
        compiler_params=_cparams("parallel"), name="ffn")(*args)


_NAT_W = (512, 1024, DT_PAD, 256, 256, 256, 256, 256)
_NAT_Z, _NAT_XBC, _NAT_DT, _NAT_DK, _NAT_SK, _NAT_DQ, _NAT_SQ, _NAT_DV = range(8)
_T_DK, _T_DV, _T_SK, _T_SV, _T_DQ, _T_SQ = range(6)
ATT_TK = 256


def _inproj_body(x_ref, g_ref, wn_ref, wt_ref, *o_refs, sample, tk, n_carried):
    o_refs = o_refs[n_carried:]
    u = _rms(x_ref[...], g_ref[...]).astype(BF16)
    offs = [sum(_NAT_W[:i]) for i in range(len(_NAT_W))]
    nat = lambda i: _dot(u, wn_ref[:, offs[i]:offs[i] + _NAT_W[i]])
    tr = lambda j: _dot_nt(wt_ref[j * ATT_W:(j + 1) * ATT_W, :], u)
    o = list(o_refs)
    for i in (_NAT_Z, _NAT_XBC, _NAT_DT):
        o.pop(0)[...] = nat(i)
    kv_t = [tr(j) for j in (_T_DK, _T_DV, _T_SK, _T_SV)]
    for v in kv_t:
        o.pop(0)[...] = v
    if sample:
        for i in (_NAT_DQ, _NAT_SQ, _NAT_DK, _NAT_DV):
            o.pop(0)[...] = nat(i)
    else:
        o.pop(0)[...] = tr(_T_DQ)
        o.pop(0)[...] = tr(_T_SQ)
        o.pop(0)[...] = nat(_NAT_DK).astype(BF16)
        o.pop(0)[...] = nat(_NAT_SK).astype(BF16)
        for v in (kv_t[1], kv_t[3]):
            r = o.pop(0)
            for c in range(v.shape[1] // tk):
                r[c] = v[:, c * tk:(c + 1) * tk].astype(BF16)
    assert not o


def _inproj(x, g, w_nat, w_t, batch, seq, sample, layer, depth, kv_carried):
    t = x.shape[0]
    tm = min(512, seq)
    tk = min(ATT_TK, seq)
    assert seq % tm == 0 and tm % tk == 0
    nj = seq // tm
    rows = lambda w: pl.BlockSpec((tm, w), lambda i: (i, 0))
    tspec = pl.BlockSpec((None, ATT_W, tm), lambda i: (i // nj, 0, i % nj))
    tshape = jax.ShapeDtypeStruct((batch, ATT_W, seq), F32)
    kvspec = pl.BlockSpec((None, None, ATT_W, tm), lambda i: (layer, i // nj, 0, i % nj))
    kvshape = jax.ShapeDtypeStruct((depth, batch, ATT_W, seq), F32)
    out_specs = [rows(_NAT_W[i]) for i in (_NAT_Z, _NAT_XBC, _NAT_DT)] + [kvspec] * 4
    out_shape = [jax.ShapeDtypeStruct((t, _NAT_W[i]), F32) for i in (_NAT_Z, _NAT_XBC, _NAT_DT)] + [kvshape] * 4
    carried = list(kv_carried) if kv_carried is not None else []
    n_fixed = 4
    aliases = {n_fixed + i: 3 + i for i in range(len(carried))}
    if sample:
        out_specs += [rows(ATT_W)] * 4
        out_shape += [jax.ShapeDtypeStruct((t, ATT_W), F32)] * 4
    else:
        out_specs += [tspec] * 2 + [rows(ATT_W)] * 2
        out_shape += [tshape] * 2 + [jax.ShapeDtypeStruct((t, ATT_W), BF16)] * 2
        out_specs += [pl.BlockSpec((None, tm // tk, ATT_W, tk), lambda i: (i // nj, i % nj, 0, 0))] * 2
        out_shape += [jax.ShapeDtypeStruct((batch, seq // tk, ATT_W, tk), BF16)] * 2
    return pl.pallas_call(
        functools.partial(_inproj_body, sample=sample, tk=tk, n_carried=len(carried)), grid=(t // tm,),
        in_specs=[rows(D_MODEL), _const_spec((1, D_MODEL)), _const_spec(w_nat.shape), _const_spec(w_t.shape)]
        + [pl.BlockSpec(memory_space=pl.ANY)] * len(carried),
        out_specs=out_specs, out_shape=out_shape, input_output_aliases=aliases,
        compiler_params=_cparams("parallel"), name="inproj")(x, g, w_nat, w_t, *carried)


def _lam(lq1, lk1, lq2, lk2, lam_init):
    s1 = jnp.sum(lq1[...] * lk1[...], axis=-1, keepdims=True)
    s2 = jnp.sum(lq2[...] * lk2[...], axis=-1, keepdims=True)
    return jnp.exp(s1) - jnp.exp(s2) + lam_init


def _attn_specs(gain_args, batch, seq, tq):
    nq = seq // tq
    small = lambda a: pl.BlockSpec(a.shape, lambda b, i: (0, 0))
    in_specs = [small(a) for a in gain_args] + [
        pl.BlockSpec((None, ATT_W, tq), lambda b, i: (b, 0, i)),
        pl.BlockSpec((seq, ATT_W), lambda b, i: (b, 0)),
        pl.BlockSpec((None, nq, ATT_W, tq), lambda b, i: (b, 0, 0, 0))]
    out_spec = pl.BlockSpec((tq, ATT_W), lambda b, i: (b * nq + i, 0))
    return in_specs, out_spec


def _norm_heads_t(o_t, gcol_ref, scale):
    parts = []
    for h in range(4):
        rows = slice(h * HEAD_W, (h + 1) * HEAD_W)
        oh = o_t[rows, :]
        ms = jnp.mean(oh * oh, axis=0, keepdims=True)
        parts.append(oh * lax.rsqrt(ms + EPS) * (gcol_ref[rows, :] * scale))
    return jnp.concatenate(parts, axis=0).T


def _diff_attn_body(lq1, lk1, lq2, lk2, gcol_ref, q_ref, k_ref, v_ref, o_ref,
                    qs_ref, m_ref, l_ref, acc_ref, *, tq, lam_init):
    qi = pl.program_id(1)
    q = q_ref[...] * (DIFF_QK ** -0.5 * LOG2E)
    seg = lax.broadcasted_iota(jnp.int32, (ATT_W, tq), 0) >> 5
    for j in range(8):
        qs_ref[j] = jnp.where(seg == j, q, 0.0).astype(BF16)
    m_ref[...] = jnp.full((8, tq), NEG, F32)
    l_ref[...] = jnp.zeros((8, tq), F32)
    acc_ref[...] = jnp.zeros((2, ATT_W, tq), F32)
    ones_rows = jnp.ones((16, tq), BF16)

    def step(kj, masked):
        start = pl.multiple_of(kj * tq, tq)
        kblk = k_ref[pl.ds(start, tq), :]
        if masked:
            keep = (lax.broadcasted_iota(jnp.int32, (tq, tq), 0) <= lax.broadcasted_iota(jnp.int32, (tq, tq), 1))
        scores = [_dot(kblk, qs_ref[j]) for j in range(8)]
        m_all, l_all = m_ref[...], l_ref[...]
        m_rows, l_rows = [], []
        for j in range(8):
            h, mi = j // 2, j % 2
            rows = slice(h * HEAD_W, (h + 1) * HEAD_W)
            s = scores[j]
            if masked:
                s = jnp.where(keep, s, NEG)
            m_prev = m_all[j:j + 1, :]
            m_new = jnp.maximum(m_prev, jnp.max(s, axis=0, keepdims=True))
            alpha = jnp.exp2(m_prev - m_new)
            p = jnp.exp2(s - m_new)
            pv = _dot(jnp.concatenate([v_ref[kj, rows, :], ones_rows], axis=0), p.astype(BF16))
            l_rows.append(alpha * l_all[j:j + 1, :] + pv[HEAD_W:HEAD_W + 1, :])
            m_rows.append(m_new)
            acc_ref[mi, rows, :] = acc_ref[mi, rows, :] * alpha + pv[:HEAD_W, :]
        m_ref[...] = jnp.concatenate(m_rows, axis=0)
        l_ref[...] = jnp.concatenate(l_rows, axis=0)

    def loop_body(i, carry):
        step(2 * i, False)
        step(2 * i + 1, False)
        return carry

    lax.fori_loop(0, qi >> 1, loop_body, 0)

    @pl.when((qi & 1) == 1)
    def _():
        step(qi - 1, False)

    step(qi, True)

    lam = _lam(lq1, lk1, lq2, lk2, lam_init)
    linv = 1.0 / l_ref[...]
    parts = []
    for h in range(4):
        rows = slice(h * HEAD_W, (h + 1) * HEAD_W)
        parts.append(acc_ref[0, rows, :] * linv[2 * h:2 * h + 1, :]
                     - lam * (acc_ref[1, rows, :] * linv[2 * h + 1:2 * h + 2, :]))
    o_ref[...] = _norm_heads_t(jnp.concatenate(parts, axis=0), gcol_ref, 1.0 - lam_init)


def _diff_attn(q_t, k, v_t, lam_params, gcol, batch, seq, lam_init):
    tq = min(ATT_TK, seq)
    assert seq % tq == 0
    in_specs, out_spec = _attn_specs(list(lam_params) + [gcol], batch, seq, tq)
    return pl.pallas_call(
        functools.partial(_diff_attn_body, tq=tq, lam_init=lam_init),
        grid=(batch, seq // tq), in_specs=in_specs, out_specs=out_spec,
        out_shape=jax.ShapeDtypeStruct((batch * seq, ATT_W), F32),
        scratch_shapes=[pltpu.VMEM((8, ATT_W, tq), BF16), pltpu.VMEM((8, tq), F32),
                        pltpu.VMEM((8, tq), F32), pltpu.VMEM((2, ATT_W, tq), F32)],
        compiler_params=_cparams("parallel", "arbitrary"), name="diff_attn")(*lam_params, gcol, q_t, k, v_t)


def _sb_attn_body(gcol_ref, q_ref, k_ref, v_ref, o_ref, qs_ref, c_ref, acc_ref, *, tq):
    qi = pl.program_id(1)
    q = q_ref[...] * (HEAD_W ** -0.5 * LOG2E)
    hrow = lax.broadcasted_iota(jnp.int32, (ATT_W, tq), 0) >> 6
    for h in range(4):
        qs_ref[h] = jnp.where(hrow == h, q, 0.0).astype(BF16)
    c_ref[...] = jnp.zeros((8, tq), F32)
    acc_ref[...] = jnp.zeros((ATT_W, tq), F32)
    ki = lax.broadcasted_iota(jnp.int32, (tq, tq), 0)
    kl = lax.broadcasted_iota(jnp.int32, (tq, tq), 1)
    later = jnp.where(kl >= ki, 1.0, 0.0).astype(BF16)
    later2 = jnp.concatenate([later, later], axis=1)

    def step(kj, masked):
        start = pl.multiple_of(kj * tq, tq)
        kblk = k_ref[pl.ds(start, tq), :]
        if masked:
            keep = ki < kl
        zs = [_dot(kblk, qs_ref[h]) for h in range(4)]
        incls = []
        for h in range(4):
            sp = _softplus2(zs[h])
            if masked:
                sp = jnp.where(keep, sp, 0.0)
            hi, lo = _split2(sp)
            incls.append(_dot(later2, jnp.concatenate([hi, lo], axis=0)))
        c_all = c_ref[...]
        c_rows = []
        for h in range(4):
            rows = slice(h * HEAD_W, (h + 1) * HEAD_W)
            c = c_all[h:h + 1, :]
            a = jnp.exp2(zs[h] - (c + incls[h]))
            if masked:
                a = jnp.where(keep, a, 0.0)
            c_rows.append(c + incls[h][0:1, :])
            acc_ref[rows, :] = acc_ref[rows, :] + _dot(v_ref[kj, rows, :], a.astype(BF16))
        c_ref[...] = jnp.concatenate(c_rows + [c_all[4:8, :]], axis=0)

    step(qi, True)

    def loop_body(i, carry):
        step(qi - 1 - 2 * i, False)
        step(qi - 2 - 2 * i, False)
        return carry

    lax.fori_loop(0, qi >> 1, loop_body, 0)

    @pl.when((qi & 1) == 1)
    def _():
        step(0, False)

    o_ref[...] = _norm_heads_t(acc_ref[...], gcol_ref, 1.0)


def _sb_attn(q_t, k, v_t, gcol, batch, seq):
    tq = min(ATT_TK, seq)
    assert seq % tq == 0
    in_specs, out_spec = _attn_specs([gcol], batch, seq, tq)
    return pl.pallas_call(
        functools.partial(_sb_attn_body, tq=tq),
        grid=(batch, seq // tq), in_specs=in_specs, out_specs=out_spec,
        out_shape=jax.ShapeDtypeStruct((batch * seq, ATT_W), F32),
        scratch_shapes=[pltpu.VMEM((4, ATT_W, tq), BF16), pltpu.VMEM((8, tq), F32),
                        pltpu.VMEM((ATT_W, tq), F32)],
        compiler_params=_cparams("parallel", "arbitrary"), name="sb_attn")(gcol, q_t, k, v_t)


def _expand_mat():
    r = lax.broadcasted_iota(jnp.int32, (DT_PAD, SSD_W), 0)
    c = lax.broadcasted_iota(jnp.int32, (DT_PAD, SSD_W), 1) >> 6
    return jnp.where(r == c, 1.0, 0.0).astype(BF16)


def _a_row(alog_ref):
    lane = lax.broadcasted_iota(jnp.int32, (1, DT_PAD), 1)
    return jnp.where(lane < SSD_HEADS, -jnp.exp(alog_ref[...]), 0.0)


def _gate_norm(y, z, nw):
    y = y * (z * jax.nn.sigmoid(z))
    half = SSD_W // 2
    parts = [_rms(y[:, g * half:(g + 1) * half], nw[:, g * half:(g + 1) * half]) for g in range(2)]
    return jnp.concatenate(parts, axis=1)


def _ssd_body(z_ref, xbc_ref, dt_ref, cw_ref, cb_ref, dtb_ref, alog_ref, dfull_ref, nw_ref,
              y_ref, hout_ref, cout_ref, ext_ref, ht_ref):
    c = pl.program_id(1)
    q = SSD_CHUNK

    @pl.when(c == 0)
    def _():
        ext_ref[0:8, :] = jnp.zeros((8, CONV_CH), F32)
        ht_ref[...] = jnp.zeros_like(ht_ref)

    ext_ref[8:8 + q, :] = xbc_ref[...]
    conv = cb_ref[...]
    for k in range(CONV_K):
        conv = conv + cw_ref[k:k + 1, :] * ext_ref[5 + k:5 + k + q, :]
    tail = ext_ref[q + 5:q + 8, :]
    ext_ref[5:8, :] = tail
    xc = conv * jax.nn.sigmoid(conv)
    xs = xc[:, :SSD_W]
    dt = _softplus(dt_ref[...] + dtb_ref[...])
    dta = dt * _a_row(alog_ref)
    ti = lax.broadcasted_iota(jnp.int32, (q, q), 0)
    si = lax.broadcasted_iota(jnp.int32, (q, q), 1)
    causal = ti >= si
    cum = _dot_sel_l(jnp.where(causal, 1.0, 0.0).astype(BF16), dta)
    emat = _expand_mat()
    cum_full = _dot_sel_r(cum, emat)
    xdt = xs * _dot_sel_r(dt, emat)
    cum_last = cum_full[q - 1:q, :]
    xdec = xdt * jnp.exp(cum_last - cum_full)
    exp_cum = jnp.exp(cum_full)
    cum_t = cum.T
    lane = lax.broadcasted_iota(jnp.int32, (1, 128), 1)
    y_slabs = []
    for g in range(2):
        bm = xc[:, SSD_W + g * SSD_STATE:SSD_W + (g + 1) * SSD_STATE]
        cm = xc[:, SSD_W + 2 * SSD_STATE + g * SSD_STATE:SSD_W + 2 * SSD_STATE + (g + 1) * SSD_STATE]
        bmb, cmb = bm.astype(BF16), cm.astype(BF16)
        cb = _dot_nt(cmb, bmb)
        bm_t = bm.T.astype(BF16)
        for j in (2 * g, 2 * g + 1):
            sl = slice(j * 128, (j + 1) * 128)
            xdt_s = xdt[:, sl]
            y_in = None
            for hh in range(2):
                h = 2 * j + hh
                segd = cum[:, h:h + 1] - cum_t[h:h + 1, :]
                w = (cb * jnp.exp(jnp.where(causal, segd, NEG))).astype(BF16)
                xh = jnp.where((lane >> 6) == hh, xdt_s, 0.0).astype(BF16)
                d = _dot(w, xh)
                y_in = d if y_in is None else y_in + d
            ht = ht_ref[j]
            y_x = _dot(cmb, ht.astype(BF16)) * exp_cum[:, sl]
            st = _dot(bm_t, xdec[:, sl].astype(BF16))
            ht_ref[j] = jnp.exp(cum_last[:, sl]) * ht + st
            y_slabs.append(y_in + y_x + dfull_ref[:, sl] * xs[:, sl])
    y = jnp.concatenate(y_slabs, axis=1)
    y_ref[...] = _gate_norm(y, z_ref[...], nw_ref[...])

    @pl.when(c == pl.num_programs(1) - 1)
    def _():
        for j in range(4):
            hout_ref[0, 2 * j:2 * j + 2] = ht_ref[j].T.reshape(2, HEAD_W, SSD_STATE)
        cout_ref[0] = tail


def _ssd_prompt(z, xbc, dt, cw, cb, dtb, alog, dfull, nw, batch, seq):
    q = SSD_CHUNK
    nc = seq // q
    rowspec = lambda w: pl.BlockSpec((q, w), lambda b, c: (b * nc + c, 0))
    small = lambda a: pl.BlockSpec(a.shape, lambda b, c: (0, 0))
    return pl.pallas_call(
        _ssd_body, grid=(batch, nc),
        in_specs=[rowspec(SSD_W), rowspec(CONV_CH), rowspec(DT_PAD)] + [small(a) for a in (cw, cb, dtb, alog, dfull, nw)],
        out_specs=[rowspec(SSD_W),
                   pl.BlockSpec((1, SSD_HEADS, HEAD_W, SSD_STATE), lambda b, c: (b, 0, 0, 0)),
                   pl.BlockSpec((1, CONV_K - 1, CONV_CH), lambda b, c: (b, 0, 0))],
        out_shape=[jax.ShapeDtypeStruct((batch * seq, SSD_W), F32),
                   jax.ShapeDtypeStruct((batch, SSD_HEADS, HEAD_W, SSD_STATE), F32),
                   jax.ShapeDtypeStruct((batch, CONV_K - 1, CONV_CH), F32)],
        scratch_shapes=[pltpu.VMEM((q + 8, CONV_CH), F32), pltpu.VMEM((4, SSD_STATE, 128), F32)],
        compiler_params=_cparams("parallel", "arbitrary"), name="ssd_prompt")(z, xbc, dt, cw, cb, dtb, alog, dfull, nw)


_SB = 8


def _dec_ssd_body(z_ref, xbc_ref, dt_ref, cs_ref, h_ref, cw_ref, cb_ref, dtb_ref, alog_ref, dfull_ref, nw_ref,
                  *rest):
    y_ref, hout_ref, cout_ref, yrow_ref = rest[-4:]
    x_new = xbc_ref[...]
    c0, c1, c2 = cs_ref[0], cs_ref[1], cs_ref[2]
    conv = cb_ref[...] + cw_ref[0:1, :] * c0 + cw_ref[1:2, :] * c1 + cw_ref[2:3, :] * c2 + cw_ref[3:4, :] * x_new
    cout_ref[0] = c1
    cout_ref[1] = c2
    cout_ref[2] = x_new
    xc = conv * jax.nn.sigmoid(conv)
    xs = xc[:, :SSD_W]
    bmat = xc[:, SSD_W:SSD_W + 2 * SSD_STATE]
    cmat = xc[:, SSD_W + 2 * SSD_STATE:]
    dt = _softplus(dt_ref[...] + dtb_ref[...])
    da = jnp.exp(dt * _a_row(alog_ref))
    emat = _expand_mat()
    xdt = xs * _dot_sel_r(dt, emat)
    da_full = _dot_sel_r(da, emat)
    x_hi, x_lo = _split2(xdt)
    d_hi = da_full.astype(BF16)
    d_r = da_full - d_hi.astype(F32)
    d_mid = d_r.astype(BF16)
    d_lo = (d_r - d_mid.astype(F32)).astype(BF16)
    g0 = lax.broadcasted_iota(jnp.int32, (_SB, SSD_W), 1) < SSD_W // 2
    zx = jnp.zeros((_SB, SSD_W), F32)
    f = lambda a: a.astype(F32)
    x0h, x0l = jnp.where(g0, f(x_hi), 0.0), jnp.where(g0, f(x_lo), 0.0)
    x1h, x1l = jnp.where(g0, 0.0, f(x_hi)), jnp.where(g0, 0.0, f(x_lo))
    xrows = [x0h, x0h, x0l, x1h, x1h, x1l, f(d_hi), f(d_mid), f(d_lo)] + [zx] * 7
    xt = jnp.concatenate(xrows, axis=0).T.astype(BF16)
    b_hi, b_lo = (f(a) for a in _split2(bmat))
    zb = jnp.zeros((_SB, SSD_STATE), F32)
    ob = jnp.ones((_SB, SSD_STATE), F32)
    cat = lambda a, b: jnp.concatenate([a, b], axis=1)
    b0h, b0l, b1h, b1l = b_hi[:, :128], b_lo[:, :128], b_hi[:, 128:], b_lo[:, 128:]
    rrows = [cat(b0h, zb), cat(b0l, zb), cat(b0h, zb), cat(b1h, zb), cat(b1l, zb), cat(b1h, zb),
             cat(zb, ob), cat(zb, ob), cat(zb, ob)] + [cat(zb, zb)] * 7
    r_all = jnp.concatenate(rrows, axis=0)
    rowi = lax.broadcasted_iota(jnp.int32, (16 * _SB, 2 * SSD_STATE), 0) & (_SB - 1)
    row8 = lax.broadcasted_iota(jnp.int32, (8, SSD_STATE), 0)
    lane512 = lax.broadcasted_iota(jnp.int32, (1, SSD_W), 1)
    for i in range(_SB):
        out = _dot(xt, jnp.where(rowi == i, r_all, 0.0).astype(BF16))
        h_new = out[:, SSD_STATE:] * h_ref[i].reshape(SSD_W, SSD_STATE) + out[:, :SSD_STATE]
        hout_ref[i] = h_new.reshape(SSD_HEADS, HEAD_W, SSD_STATE)
        crow = jnp.where(row8 == 0, cmat[i:i + 1, :SSD_STATE],
                         jnp.where(row8 == 1, cmat[i:i + 1, SSD_STATE:], 0.0))
        yy = _dot_nt(crow.astype(BF16), h_new.astype(BF16))
        yrow_ref[i:i + 1, :] = jnp.where(lane512 < SSD_W // 2, yy[0:1, :], yy[1:2, :])
    y = yrow_ref[...] + dfull_ref[...] * xs
    y_ref[...] = _gate_norm(y, z_ref[...], nw_ref[...])


def _ssd_sample(z, xbc, dt, conv_state, ssm_state, layer, cw, cb, dtb, alog, dfull, nw, h_carried):
    nb = z.shape[0]
    assert nb % _SB == 0
    rowspec = lambda w: pl.BlockSpec((_SB, w), lambda i: (i, 0))
    small = lambda a: pl.BlockSpec(a.shape, lambda i: (0, 0))
    cshape, hshape = (CONV_K - 1, _SB, CONV_CH), (_SB, SSD_HEADS, HEAD_W, SSD_STATE)
    cspec = pl.BlockSpec(cshape, lambda i: (0, i, 0))
    cin = pl.BlockSpec((None,) + cshape, lambda i: (layer, 0, i, 0))
    hin = pl.BlockSpec((None,) + hshape, lambda i: (layer, i, 0, 0, 0))
    args = [z, xbc, dt, conv_state, ssm_state, cw, cb, dtb, alog, dfull, nw]
    in_specs = [rowspec(SSD_W), rowspec(CONV_CH), rowspec(DT_PAD), cin, hin] + [small(a) for a in (cw, cb, dtb, alog, dfull, nw)]
    aliases = {}
    if h_carried is not None:
        aliases = {len(args): 1}
        args.append(h_carried)
        in_specs.append(pl.BlockSpec(memory_space=pl.ANY))
    return pl.pallas_call(
        _dec_ssd_body, grid=(nb // _SB,), in_specs=in_specs,
        out_specs=[rowspec(SSD_W), hin, cspec],
        out_shape=[jax.ShapeDtypeStruct((nb, SSD_W), F32),
                   jax.ShapeDtypeStruct(ssm_state.shape, F32),
                   jax.ShapeDtypeStruct(conv_state.shape[1:], F32)],
        input_output_aliases=aliases,
        scratch_shapes=[pltpu.VMEM((_SB, SSD_W), F32)],
        compiler_params=_cparams("parallel"), name="ssd_sample")(*args)


def _row_head_rms(res, gain, scale):
    hl = lax.broadcasted_iota(jnp.int32, (1, ATT_W), 1) >> 6
    sq = res * res
    ms = jnp.zeros_like(res)
    for h in range(4):
        ms_h = jnp.sum(jnp.where(hl == h, sq, 0.0), axis=-1, keepdims=True) * (1.0 / HEAD_W)
        ms = jnp.where(hl == h, ms_h, ms)
    return res * lax.rsqrt(ms + EPS) * gain * scale


def _dec_attn_body(pt_ref, lq1, lk1, lq2, lk2, gnd_ref, gns_ref, qd_ref, kdn_ref, vdn_ref, qs_ref, *rest,
                   n_pages, page, lam_init):
    del pt_ref
    kd, vd = rest[0:n_pages], rest[n_pages:2 * n_pages]
    ks, vs = rest[2 * n_pages:3 * n_pages], rest[3 * n_pages:4 * n_pages]
    od_ref, os_ref = rest[4 * n_pages:]
    lane8 = lax.broadcasted_iota(jnp.int32, (8, ATT_W), 1)
    row8 = lax.broadcasted_iota(jnp.int32, (8, ATT_W), 0)

    qd = qd_ref[0] * (DIFF_QK ** -0.5)
    qm = jnp.where((lane8 >> 5) == row8, qd, 0.0)
    qmb = qm.astype(BF16)
    s_pages = [_dot(qmb, kd[p][...].astype(BF16)) for p in range(n_pages)]
    s_new = jnp.sum(qm * kdn_ref[0], axis=-1, keepdims=True)
    m = s_new
    for s in s_pages:
        m = jnp.maximum(m, jnp.max(s, axis=-1, keepdims=True))
    l = jnp.exp(s_new - m)
    o = l * vdn_ref[0]
    for p in range(n_pages):
        pr = jnp.exp(s_pages[p] - m)
        l = l + jnp.sum(pr, axis=-1, keepdims=True)
        o = o + _dot_nt(pr.astype(BF16), vd[p][...].astype(BF16))
    o = o / l
    lam = _lam(lq1, lk1, lq2, lk2, lam_init)
    coef = jnp.where((lane8 >> 6) == (row8 >> 1), jnp.where((row8 & 1) == 0, 1.0, -lam), 0.0)
    res = jnp.sum(coef * o, axis=0, keepdims=True)
    od_ref[0] = _row_head_rms(res, gnd_ref[...], 1.0 - lam_init)

    qs = qs_ref[0] * (HEAD_W ** -0.5)
    qsm = jnp.where((lane8 >> 6) == row8, qs, 0.0).astype(BF16)
    z = jnp.concatenate([_dot(qsm, ks[p][...].astype(BF16)) for p in range(n_pages)], axis=0)
    r = 8 * n_pages
    sp = _softplus(z)
    log_1m = -sp
    jj = lax.broadcasted_iota(jnp.int32, (page, page), 0)
    ss = lax.broadcasted_iota(jnp.int32, (page, page), 1)
    uincl = jnp.where(jj >= ss, 1.0, 0.0).astype(BF16)
    hi, lo = _split2(log_1m)
    incl = _dot(hi, uincl) + _dot(lo, uincl)
    tot = jnp.broadcast_to(jnp.sum(log_1m, axis=-1, keepdims=True), (r, page))
    ri = lax.broadcasted_iota(jnp.int32, (r, r), 0)
    ci = lax.broadcasted_iota(jnp.int32, (r, r), 1)
    upage = jnp.where(((ri & 7) == (ci & 7)) & ((ci >> 3) > (ri >> 3)), 1.0, 0.0).astype(BF16)
    thi, tlo = _split2(tot)
    carry = _dot(upage, thi) + _dot(upage, tlo)
    a = jnp.exp(z - sp + (incl - log_1m) + carry).astype(BF16)
    osb = None
    for p in range(n_pages):
        d = _dot_nt(a[8 * p:8 * p + 8, :], vs[p][...].astype(BF16))
        osb = d if osb is None else osb + d
    res_s = jnp.sum(jnp.where((lane8 >> 6) == row8, osb, 0.0), axis=0, keepdims=True)
    os_ref[0] = _row_head_rms(res_s, gns_ref[...], 1.0)


def _dec_attn(page_flat, lam_params, gnd, gns, qd, kdn, vdn, qs, caches, layer, n_pages, lam_init):
    nb = qd.shape[0]
    page = caches[0].shape[3]
    assert page == 128
    small = lambda a: pl.BlockSpec(a.shape, lambda b, pt: (0, 0))
    rowspec = pl.BlockSpec((1, 1, ATT_W), lambda b, pt: (b, 0, 0))
    page_specs = []
    for _ in caches:
        for p in range(n_pages):
            page_specs.append(pl.BlockSpec((None, None, ATT_W, page),
                                           lambda b, pt, p=p: (layer, pt[b * n_pages + p], 0, 0)))
    page_args = [c for c in caches for _ in range(n_pages)]
    grid_spec = pltpu.PrefetchScalarGridSpec(
        num_scalar_prefetch=1, grid=(nb,),
        in_specs=[small(a) for a in lam_params] + [small(gnd), small(gns), rowspec, rowspec, rowspec, rowspec] + page_specs,
        out_specs=[rowspec, rowspec])
    return pl.pallas_call(
        functools.partial(_dec_attn_body, n_pages=n_pages, page=page, lam_init=lam_init),
        grid_spec=grid_spec,
        out_shape=[jax.ShapeDtypeStruct((nb, 1, ATT_W), F32)] * 2,
        compiler_params=_cparams("parallel"), name="dec_attn")(
            page_flat, *lam_params, gnd, gns, qd, kdn, vdn, qs, *page_args)


def _split_cols(w):
    outs, i = [], 0
    for n in IN_SIZES:
        outs.append(w[:, i:i + n])
        i += n
    return outs


def _pack_w_in(w):
    dq, dk, dv, z, xbc, dt, sq, sk, sv = _split_cols(w)
    dt = jnp.pad(dt, ((0, 0), (0, DT_PAD - dt.shape[1])))
    w_nat = jnp.concatenate([z, xbc, dt, dk, sk, dq, sq, dv], axis=1).astype(BF16)
    w_t = jnp.concatenate([dk, dv, sk, sv, dq, sq], axis=1).T.astype(BF16)
    return w_nat, w_t


def kernel(x_prompt, x_sample, cache_diff_k, cache_diff_v, cache_sb_k, cache_sb_v, state_ssm, state_conv, page_table, norm_ffn1, w_ffn1_gu, w_ffn1_down, norm_mix, w_in, diff_lambda_q1, diff_lambda_k1, diff_lambda_q2, diff_lambda_k2, diff_norm, ssd_conv_w, ssd_conv_b, ssd_dt_bias, ssd_a_log, ssd_d, ssd_norm, sb_norm, w_out, norm_ffn2, w_ffn2_gu, w_ffn2_down, norm_final):
    batch, seq, _ = x_prompt.shape
    nb = x_sample.shape[0]
    depth = w_in.shape[0]
    n_pages = page_table.shape[1]
    n_pool, page = cache_diff_k.shape[1], cache_diff_k.shape[2]
    tp = batch * seq
    xp = x_prompt.reshape(tp, D_MODEL)
    xs = x_sample.reshape(nb, D_MODEL)
    caches = [jnp.transpose(c, (0, 1, 3, 4, 2)).reshape(depth, n_pool, ATT_W, page)
              for c in (cache_diff_k, cache_diff_v, cache_sb_k, cache_sb_v)]
    conv_in = jnp.transpose(state_conv, (0, 2, 1, 3))
    page_flat = page_table.reshape(-1).astype(jnp.int32)
    row = lambda a: a.reshape(1, -1).astype(F32)
    gfinal = row(norm_final)
    new_p, new_s = [], []
    kv_p = kv_s = h_s = None
    for l in range(depth):
        lam_init = 0.8 - 0.6 * math.exp(-0.3 * l)
        last = l == depth - 1
        wg1, wu1 = w_ffn1_gu[l][:, :D_FF].astype(BF16), w_ffn1_gu[l][:, D_FF:].astype(BF16)
        wd1 = w_ffn1_down[l].astype(BF16)
        wg2, wu2 = w_ffn2_gu[l][:, :D_FF].astype(BF16), w_ffn2_gu[l][:, D_FF:].astype(BF16)
        wd2 = w_ffn2_down[l].astype(BF16)
        w_nat, w_t = _pack_w_in(w_in[l])
        w_out_b = w_out[l].astype(BF16)
        lam_params = [row(a[l]) for a in (diff_lambda_q1, diff_lambda_k1, diff_lambda_q2, diff_lambda_k2)]
        gnd = row(jnp.tile(diff_norm[l], 4))
        gns = row(jnp.tile(sb_norm[l], 4))
        cw, cb = ssd_conv_w[l].astype(F32), row(ssd_conv_b[l])
        dtb = row(jnp.pad(ssd_dt_bias[l], (0, DT_PAD - SSD_HEADS)))
        alog = row(jnp.pad(ssd_a_log[l], (0, DT_PAD - SSD_HEADS)))
        dfull = row(jnp.repeat(ssd_d[l], HEAD_W))
        nw = row(ssd_norm[l])
        g1, gm, g2 = row(norm_ffn1[l]), row(norm_mix[l]), row(norm_ffn2[l])

        xp = _ffn(xp, g1, wg1, wu1, wd1)
        z, xbc, dt, *rest = _inproj(xp, gm, w_nat, w_t, batch, seq, False, l, depth, kv_p)
        kv_p, (dq_t, sq_t, dk_a, sk_a, dv_a, sv_a) = rest[:4], rest[4:]
        od = _diff_attn(dq_t, dk_a, dv_a, lam_params, gnd.reshape(ATT_W, 1), batch, seq, lam_init)
        os_ = _sb_attn(sq_t, sk_a, sv_a, gns.reshape(ATT_W, 1), batch, seq)
        y, h_p, conv_p = _ssd_prompt(z, xbc, dt, cw, cb, dtb, alog, dfull, nw, batch, seq)
        xp = _ffn(xp, g2, wg2, wu2, wd2, gfinal if last else None, mix=(od, y, os_, w_out_b))
        new_p.append((h_p, conv_p))

        xs = _ffn(xs, g1, wg1, wu1, wd1)
        z, xbc, dt, *rest = _inproj(xs, gm, w_nat, w_t, 1, nb, True, l, depth, kv_s)
        kv_s, (dq, sq, dk_n, dv_n) = rest[:4], rest[4:]
        r3 = lambda a: a.reshape(nb, 1, ATT_W)
        od, os_ = _dec_attn(page_flat, lam_params, gnd, gns, r3(dq), r3(dk_n), r3(dv_n), r3(sq), caches, l,
                            n_pages, lam_init)
        y, h_s, conv_s = _ssd_sample(z, xbc, dt, conv_in, state_ssm, l, cw, cb, dtb, alog, dfull, nw, h_s)
        xs = _ffn(xs, g2, wg2, wu2, wd2, gfinal if last else None,
                  mix=(od.reshape(nb, ATT_W), y, os_.reshape(nb, ATT_W), w_out_b))
        new_s.append(conv_s)

    kv_p_out = [jnp.transpose(a.reshape(depth, batch, 4, HEAD_W, seq), (0, 1, 4, 2, 3)) for a in kv_p]
    kv_s_out = [jnp.transpose(a.reshape(depth, 4, HEAD_W, nb), (0, 3, 1, 2)).reshape(depth, nb, 1, 4, HEAD_W)
                for a in kv_s]
    conv_s_out = jnp.transpose(jnp.stack(new_s, axis=0), (0, 2, 1, 3))
    return (xp.reshape(batch, seq, D_MODEL), xs.reshape(nb, 1, D_MODEL), *kv_p_out,
            jnp.stack([p[0] for p in new_p], axis=0), jnp.stack([p[1] for p in new_p], axis=0),
            *kv_s_out, h_s, conv_s_out)
```

```python
import functools
import math

import jax
import jax.numpy as jnp
from jax import lax
from jax.experimental import pallas as pl
from jax.experimental.pallas import tpu as pltpu

F32, BF16 = jnp.float32, jnp.bfloat16

D_MODEL = 1024
D_FF = 2816
DIFF_QK = 32
HEAD_W = 64
ATT_W = 256
SSD_W = 512
SSD_HEADS = 8
SSD_STATE = 128
SSD_CHUNK = 128
CONV_K = 4
CONV_CH = 1024
EPS = 1e-6
NEG = -1e30
IN_SIZES = (256, 256, 256, 512, 1024, 8, 256, 256, 256)
DT_PAD = 128
FF_CHUNK = 256
VMEM_LIMIT = 56 * 1024 * 1024


def _cparams(*sem):
    return pltpu.CompilerParams(dimension_semantics=sem, vmem_limit_bytes=VMEM_LIMIT)


def _rms(x, g):
    return x * lax.rsqrt(jnp.mean(x * x, axis=-1, keepdims=True) + EPS) * g


def _softplus(x):
    return jnp.maximum(x, 0.0) + jnp.log1p(jnp.exp(-jnp.abs(x)))


LOG2E = 1.4426950408889634


def _softplus2(x):
    return jnp.maximum(x, 0.0) + jnp.log2(1.0 + jnp.exp2(-jnp.abs(x)))


def _dot(a, b):
    return jnp.dot(a, b, preferred_element_type=F32)


def _dot_nt(a, b):
    return lax.dot_general(a, b, (((1,), (1,)), ((), ())), preferred_element_type=F32)


def _split2(x):
    hi = x.astype(BF16)
    lo = (x - hi.astype(F32)).astype(BF16)
    return hi, lo


def _split3(x):
    hi = x.astype(BF16)
    r = x - hi.astype(F32)
    mid = r.astype(BF16)
    return hi, mid, (r - mid.astype(F32)).astype(BF16)


def _dot_sel_l(sel, x):
    return sum(_dot(sel, p) for p in _split3(x))


def _dot_sel_r(x, sel):
    return sum(_dot(p, sel) for p in _split3(x))


def _const_spec(shape):
    nd = len(shape)
    return pl.BlockSpec(shape, lambda *_: (0,) * nd, pipeline_mode=pl.Buffered(1))


def _ffn_body(*refs, n_chunks, final, mix):
    refs = list(refs)
    x_ref, g_ref, wg_ref, wu_ref, wd_ref = refs[:5]
    o_ref = refs.pop()
    gf_ref = refs.pop() if final else None
    x = x_ref[...]
    if mix:
        od_ref, y_ref, os_ref, wo_ref = refs[5:9]
        x = x + _dot(od_ref[...].astype(BF16), wo_ref[0:ATT_W, :])
        x = x + _dot(y_ref[...].astype(BF16), wo_ref[ATT_W:ATT_W + SSD_W, :])
        x = x + _dot(os_ref[...].astype(BF16), wo_ref[ATT_W + SSD_W:, :])
    xn = _rms(x, g_ref[...]).astype(BF16)
    acc = jnp.zeros_like(x)
    for c in range(n_chunks):
        sl = slice(c * FF_CHUNK, (c + 1) * FF_CHUNK)
        gate = _dot(xn, wg_ref[:, sl])
        up = _dot(xn, wu_ref[:, sl])
        h = (gate * jax.nn.sigmoid(gate) * up).astype(BF16)
        acc = acc + _dot(h, wd_ref[sl, :])
    y = x + 0.5 * acc
    if final:
        y = _rms(y, gf_ref[...])
    o_ref[...] = y


def _ffn(x, g, wg, wu, wd, final_g=None, mix=None):
    t = x.shape[0]
    tm = min(512, t)
    assert t % tm == 0 and D_FF % FF_CHUNK == 0
    rows = lambda w: pl.BlockSpec((tm, w), lambda i: (i, 0))
    row = rows(D_MODEL)
    in_specs = [row, _const_spec((1, D_MODEL)), _const_spec((D_MODEL, D_FF)), _const_spec((D_MODEL, D_FF)),
                _const_spec((D_FF, D_MODEL))]
    args = [x, g, wg, wu, wd]
    if mix is not None:
        in_specs += [rows(ATT_W), rows(SSD_W), rows(ATT_W), _const_spec((D_MODEL, D_MODEL))]
        args += list(mix)
    if final_g is not None:
        in_specs.append(_const_spec((1, D_MODEL)))
        args.append(final_g)
    return pl.pallas_call(
        functools.partial(_ffn_body, n_chunks=D_FF // FF_CHUNK, final=final_g is not None, mix=mix is not None),
        grid=(t // tm,), in_specs=in_specs, out_specs=row,
        out_shape=jax.ShapeDtypeStruct((t, D_MODEL), F32),
        compiler_params=_cparams("parallel"), name="ffn")(*args)


_NAT_W = (512, 1024, DT_PAD, 256, 256, 256, 256, 256)
_NAT_Z, _NAT_XBC, _NAT_DT, _NAT_DK, _NAT_SK, _NAT_DQ, _NAT_SQ, _NAT_DV = range(8)
_T_DK, _T_DV, _T_SK, _T_SV, _T_DQ, _T_SQ = range(6)
ATT_TK = 256


def _inproj_body(x_ref, g_ref, wn_ref, wt_ref, *o_refs, sample, tk, n_carried):
    o_refs = o_refs[n_carried:]
    u = _rms(x_ref[...], g_ref[...]).astype(BF16)
    offs = [sum(_NAT_W[:i]) for i in range(len(_NAT_W))]
    nat = lambda i: _dot(u, wn_ref[:, offs[i]:offs[i] + _NAT_W[i]])
    tr = lambda j: _dot_nt(wt_ref[j * ATT_W:(j + 1) * ATT_W, :], u)
    o = list(o_refs)
    for i in (_NAT_Z, _NAT_XBC, _NAT_DT):
        o.pop(0)[...] = nat(i)
    kv_t = [tr(j) for j in (_T_DK, _T_DV, _T_SK, _T_SV)]
    for v in kv_t:
        o.pop(0)[...] = v
    if sample:
        for i in (_NAT_DQ, _NAT_SQ, _NAT_DK, _NAT_DV):
            o.pop(0)[...] = nat(i)
    else:
        o.pop(0)[...] = tr(_T_DQ)
        o.pop(0)[...] = tr(_T_SQ)
        o.pop(0)[...] = nat(_NAT_DK).astype(BF16)
        o.pop(0)[...] = nat(_NAT_SK).astype(BF16)
        for v in (kv_t[1], kv_t[3]):
            r = o.pop(0)
            for c in range(v.shape[1] // tk):
                r[c] = v[:, c * tk:(c + 1) * tk].astype(BF16)
    assert not o


def _inproj(x, g, w_nat, w_t, batch, seq, sample, layer, depth, kv_carried):
    t = x.shape[0]
    tm = min(512, seq)
    tk = min(ATT_TK, seq)
    assert seq % tm == 0 and tm % tk == 0
    nj = seq // tm
    rows = lambda w: pl.BlockSpec((tm, w), lambda i: (i, 0))
    tspec = pl.BlockSpec((None, ATT_W, tm), lambda i: (i // nj, 0, i % nj))
    tshape = jax.ShapeDtypeStruct((batch, ATT_W, seq), F32)
    kvspec = pl.BlockSpec((None, None, ATT_W, tm), lambda i: (layer, i // nj, 0, i % nj))
    kvshape = jax.ShapeDtypeStruct((depth, batch, ATT_W, seq), F32)
    out_specs = [rows(_NAT_W[i]) for i in (_NAT_Z, _NAT_XBC, _NAT_DT)] + [kvspec] * 4
    out_shape = [jax.ShapeDtypeStruct((t, _NAT_W[i]), F32) for i in (_NAT_Z, _NAT_XBC, _NAT_DT)] + [kvshape] * 4
    carried = list(kv_carried) if kv_carried is not None else []
    n_fixed = 4
    aliases = {n_fixed + i: 3 + i for i in range(len(carried))}
    if sample:
        out_specs += [rows(ATT_W)] * 4
        out_shape += [jax.ShapeDtypeStruct((t, ATT_W), F32)] * 4
    else:
        out_specs += [tspec] * 2 + [rows(ATT_W)] * 2
        out_shape += [tshape] * 2 + [jax.ShapeDtypeStruct((t, ATT_W), BF16)] * 2
        out_specs += [pl.BlockSpec((None, tm // tk, ATT_W, tk), lambda i: (i // nj, i % nj, 0, 0))] * 2
        out_shape += [jax.ShapeDtypeStruct((batch, seq // tk, ATT_W, tk), BF16)] * 2
    return pl.pallas_call(
        functools.partial(_inproj_body, sample=sample, tk=tk, n_carried=len(carried)), grid=(t // tm,),
        in_specs=[rows(D_MODEL), _const_spec((1, D_MODEL)), _const_spec(w_nat.shape), _const_spec(w_t.shape)]
        + [pl.BlockSpec(memory_space=pl.ANY)] * len(carried),
        out_specs=out_specs, out_shape=out_shape, input_output_aliases=aliases,
        compiler_params=_cparams("parallel"), name="inproj")(x, g, w_nat, w_t, *carried)


def _lam(lq1, lk1, lq2, lk2, lam_init):
    s1 = jnp.sum(lq1[...] * lk1[...], axis=-1, keepdims=True)
    s2 = jnp.sum(lq2[...] * lk2[...], axis=-1, keepdims=True)
    return jnp.exp(s1) - jnp.exp(s2) + lam_init


def _attn_specs(gain_args, batch, seq, tq):
    nq = seq // tq
    small = lambda a: pl.BlockSpec(a.shape, lambda b, i: (0, 0))
    in_specs = [small(a) for a in gain_args] + [
        pl.BlockSpec((None, ATT_W, tq), lambda b, i: (b, 0, i)),
        pl.BlockSpec((seq, ATT_W), lambda b, i: (b, 0)),
        pl.BlockSpec((None, nq, ATT_W, tq), lambda b, i: (b, 0, 0, 0))]
    out_spec = pl.BlockSpec((tq, ATT_W), lambda b, i: (b * nq + i, 0))
    return in_specs, out_spec


def _norm_heads_t(o_t, gcol_ref, scale):
    parts = []
    for h in range(4):
        rows = slice(h * HEAD_W, (h + 1) * HEAD_W)
        oh = o_t[rows, :]
        ms = jnp.mean(oh * oh, axis=0, keepdims=True)
        parts.append(oh * lax.rsqrt(ms + EPS) * (gcol_ref[rows, :] * scale))
    return jnp.concatenate(parts, axis=0).T


def _diff_attn_body(lq1, lk1, lq2, lk2, gcol_ref, q_ref, k_ref, v_ref, o_ref,
                    qs_ref, m_ref, l_ref, acc_ref, sa_ref, sb_ref, *, tq, lam_init):
    qi = pl.program_id(1)
    q = q_ref[...] * (DIFF_QK ** -0.5 * LOG2E)
    seg = lax.broadcasted_iota(jnp.int32, (ATT_W, tq), 0) >> 5
    for j in range(8):
        qs_ref[j] = jnp.where(seg == j, q, 0.0).astype(BF16)
    m_ref[...] = jnp.full((8, tq), NEG, F32)
    l_ref[...] = jnp.zeros((8, tq), F32)
    acc_ref[...] = jnp.zeros((2, ATT_W, tq), F32)
    ones_rows = jnp.ones((16, tq), BF16)

    def scores(kj, s_ref):
        start = pl.multiple_of(kj * tq, tq)
        kblk = k_ref[pl.ds(start, tq), :]
        for j in range(8):
            s_ref[j] = _dot(kblk, qs_ref[j])

    def softmax_pv(kj, s_ref, masked):
        if masked:
            keep = (lax.broadcasted_iota(jnp.int32, (tq, tq), 0) <= lax.broadcasted_iota(jnp.int32, (tq, tq), 1))
        m_all, l_all = m_ref[...], l_ref[...]
        m_rows, l_rows = [], []
        for j in range(8):
            h, mi = j // 2, j % 2
            rows = slice(h * HEAD_W, (h + 1) * HEAD_W)
            s = s_ref[j]
            if masked:
                s = jnp.where(keep, s, NEG)
            m_prev = m_all[j:j + 1, :]
            m_new = jnp.maximum(m_prev, jnp.max(s, axis=0, keepdims=True))
            alpha = jnp.exp2(m_prev - m_new)
            p = jnp.exp2(s - m_new)
            pv = _dot(jnp.concatenate([v_ref[kj, rows, :], ones_rows], axis=0), p.astype(BF16))
            l_rows.append(alpha * l_all[j:j + 1, :] + pv[HEAD_W:HEAD_W + 1, :])
            m_rows.append(m_new)
            acc_ref[mi, rows, :] = acc_ref[mi, rows, :] * alpha + pv[:HEAD_W, :]
        m_ref[...] = jnp.concatenate(m_rows, axis=0)
        l_ref[...] = jnp.concatenate(l_rows, axis=0)

    scores(0, sa_ref)

    def loop_body(i, carry):
        scores(2 * i + 1, sb_ref)
        softmax_pv(2 * i, sa_ref, False)
        scores(jnp.minimum(2 * i + 2, qi), sa_ref)
        softmax_pv(2 * i + 1, sb_ref, False)
        return carry

    lax.fori_loop(0, qi >> 1, loop_body, 0)

    @pl.when((qi & 1) == 1)
    def _():
        scores(qi, sb_ref)
        softmax_pv(qi - 1, sa_ref, False)
        softmax_pv(qi, sb_ref, True)

    @pl.when((qi & 1) == 0)
    def _():
        softmax_pv(qi, sa_ref, True)

    lam = _lam(lq1, lk1, lq2, lk2, lam_init)
    linv = 1.0 / l_ref[...]
    parts = []
    for h in range(4):
        rows = slice(h * HEAD_W, (h + 1) * HEAD_W)
        parts.append(acc_ref[0, rows, :] * linv[2 * h:2 * h + 1, :]
                     - lam * (acc_ref[1, rows, :] * linv[2 * h + 1:2 * h + 2, :]))
    o_ref[...] = _norm_heads_t(jnp.concatenate(parts, axis=0), gcol_ref, 1.0 - lam_init)


def _diff_attn(q_t, k, v_t, lam_params, gcol, batch, seq, lam_init):
    tq = min(ATT_TK, seq)
    assert seq % tq == 0
    in_specs, out_spec = _attn_specs(list(lam_params) + [gcol], batch, seq, tq)
    return pl.pallas_call(
        functools.partial(_diff_attn_body, tq=tq, lam_init=lam_init),
        grid=(batch, seq // tq), in_specs=in_specs, out_specs=out_spec,
        out_shape=jax.ShapeDtypeStruct((batch * seq, ATT_W), F32),
        scratch_shapes=[pltpu.VMEM((8, ATT_W, tq), BF16), pltpu.VMEM((8, tq), F32),
                        pltpu.VMEM((8, tq), F32), pltpu.VMEM((2, ATT_W, tq), F32),
                        pltpu.VMEM((8, tq, tq), F32), pltpu.VMEM((8, tq, tq), F32)],
        compiler_params=_cparams("parallel", "arbitrary"), name="diff_attn")(*lam_params, gcol, q_t, k, v_t)


def _sb_attn_body(gcol_ref, q_ref, k_ref, v_ref, o_ref, qs_ref, c_ref, acc_ref, sa_ref, sb_ref, *, tq):
    qi = pl.program_id(1)
    q = q_ref[...] * (HEAD_W ** -0.5 * LOG2E)
    hrow = lax.broadcasted_iota(jnp.int32, (ATT_W, tq), 0) >> 6
    for h in range(4):
        qs_ref[h] = jnp.where(hrow == h, q, 0.0).astype(BF16)
    c_ref[...] = jnp.zeros((8, tq), F32)
    acc_ref[...] = jnp.zeros((ATT_W, tq), F32)
    ki = lax.broadcasted_iota(jnp.int32, (tq, tq), 0)
    kl = lax.broadcasted_iota(jnp.int32, (tq, tq), 1)
    later = jnp.where(kl >= ki, 1.0, 0.0).astype(BF16)
    later2 = jnp.concatenate([later, later], axis=1)

    def zscores(kj, z_ref):
        start = pl.multiple_of(kj * tq, tq)
        kblk = k_ref[pl.ds(start, tq), :]
        for h in range(4):
            z_ref[h] = _dot(kblk, qs_ref[h])

    def weigh_pv(kj, z_ref, masked):
        if masked:
            keep = ki < kl
        zs = [z_ref[h] for h in range(4)]
        incls = []
        for h in range(4):
            sp = _softplus2(zs[h])
            if masked:
                sp = jnp.where(keep, sp, 0.0)
            hi, lo = _split2(sp)
            incls.append(_dot(later2, jnp.concatenate([hi, lo], axis=0)))
        c_all = c_ref[...]
        c_rows = []
        for h in range(4):
            rows = slice(h * HEAD_W, (h + 1) * HEAD_W)
            c = c_all[h:h + 1, :]
            a = jnp.exp2(zs[h] - (c + incls[h]))
            if masked:
                a = jnp.where(keep, a, 0.0)
            c_rows.append(c + incls[h][0:1, :])
            acc_ref[rows, :] = acc_ref[rows, :] + _dot(v_ref[kj, rows, :], a.astype(BF16))
        c_ref[...] = jnp.concatenate(c_rows + [c_all[4:8, :]], axis=0)

    zscores(qi, sa_ref)
    zscores(jnp.maximum(qi - 1, 0), sb_ref)
    weigh_pv(qi, sa_ref, True)

    def loop_body(i, carry):
        u = qi - 1 - 2 * i
        zscores(u - 1, sa_ref)
        weigh_pv(u, sb_ref, False)
        zscores(jnp.maximum(u - 2, 0), sb_ref)
        weigh_pv(u - 1, sa_ref, False)
        return carry

    lax.fori_loop(0, qi >> 1, loop_body, 0)

    @pl.when((qi & 1) == 1)
    def _():
        weigh_pv(0, sb_ref, False)

    o_ref[...] = _norm_heads_t(acc_ref[...], gcol_ref, 1.0)


def _sb_attn(q_t, k, v_t, gcol, batch, seq):
    tq = min(ATT_TK, seq)
    assert seq % tq == 0
    in_specs, out_spec = _attn_specs([gcol], batch, seq, tq)
    return pl.pallas_call(
        functools.partial(_sb_attn_body, tq=tq),
        grid=(batch, seq // tq), in_specs=in_specs, out_specs=out_spec,
        out_shape=jax.ShapeDtypeStruct((batch * seq, ATT_W), F32),
        scratch_shapes=[pltpu.VMEM((4, ATT_W, tq), BF16), pltpu.VMEM((8, tq), F32),
                        pltpu.VMEM((ATT_W, tq), F32),
                        pltpu.VMEM((4, tq, tq), F32), pltpu.VMEM((4, tq, tq), F32)],
        compiler_params=_cparams("parallel", "arbitrary"), name="sb_attn")(gcol, q_t, k, v_t)


def _expand_mat():
    r = lax.broadcasted_iota(jnp.int32, (DT_PAD, SSD_W), 0)
    c = lax.broadcasted_iota(jnp.int32, (DT_PAD, SSD_W), 1) >> 6
    return jnp.where(r == c, 1.0, 0.0).astype(BF16)


def _a_row(alog_ref):
    lane = lax.broadcasted_iota(jnp.int32, (1, DT_PAD), 1)
    return jnp.where(lane < SSD_HEADS, -jnp.exp(alog_ref[...]), 0.0)


def _gate_norm(y, z, nw):
    y = y * (z * jax.nn.sigmoid(z))
    half = SSD_W // 2
    parts = [_rms(y[:, g * half:(g + 1) * half], nw[:, g * half:(g + 1) * half]) for g in range(2)]
    return jnp.concatenate(parts, axis=1)


def _ssd_body(z_ref, xbc_ref, dt_ref, cw_ref, cb_ref, dtb_ref, alog_ref, dfull_ref, nw_ref,
              y_ref, hout_ref, cout_ref, ext_ref, ht_ref):
    c = pl.program_id(1)
    q = SSD_CHUNK

    @pl.when(c == 0)
    def _():
        ext_ref[0:8, :] = jnp.zeros((8, CONV_CH), F32)
        ht_ref[...] = jnp.zeros_like(ht_ref)

    ext_ref[8:8 + q, :] = xbc_ref[...]
    conv = cb_ref[...]
    for k in range(CONV_K):
        conv = conv + cw_ref[k:k + 1, :] * ext_ref[5 + k:5 + k + q, :]
    tail = ext_ref[q + 5:q + 8, :]
    ext_ref[5:8, :] = tail
    xc = conv * jax.nn.sigmoid(conv)
    xs = xc[:, :SSD_W]
    dt = _softplus(dt_ref[...] + dtb_ref[...])
    dta = dt * _a_row(alog_ref)
    ti = lax.broadcasted_iota(jnp.int32, (q, q), 0)
    si = lax.broadcasted_iota(jnp.int32, (q, q), 1)
    causal = ti >= si
    cum = _dot_sel_l(jnp.where(causal, 1.0, 0.0).astype(BF16), dta)
    emat = _expand_mat()
    cum_full = _dot_sel_r(cum, emat)
    xdt = xs * _dot_sel_r(dt, emat)
    cum_last = cum_full[q - 1:q, :]
    xdec = xdt * jnp.exp(cum_last - cum_full)
    exp_cum = jnp.exp(cum_full)
    cum_t = cum.T
    lane = lax.broadcasted_iota(jnp.int32, (1, 128), 1)
    y_slabs = []
    for g in range(2):
        bm = xc[:, SSD_W + g * SSD_STATE:SSD_W + (g + 1) * SSD_STATE]
        cm = xc[:, SSD_W + 2 * SSD_STATE + g * SSD_STATE:SSD_W + 2 * SSD_STATE + (g + 1) * SSD_STATE]
        bmb, cmb = bm.astype(BF16), cm.astype(BF16)
        cb = _dot_nt(cmb, bmb)
        bm_t = bm.T.astype(BF16)
        for j in (2 * g, 2 * g + 1):
            sl = slice(j * 128, (j + 1) * 128)
            xdt_s = xdt[:, sl]
            y_in = None
            for hh in range(2):
                h = 2 * j + hh
                segd = cum[:, h:h + 1] - cum_t[h:h + 1, :]
                w = (cb * jnp.exp(jnp.where(causal, segd, NEG))).astype(BF16)
                xh = jnp.where((lane >> 6) == hh, xdt_s, 0.0).astype(BF16)
                d = _dot(w, xh)
                y_in = d if y_in is None else y_in + d
            ht = ht_ref[j]
            y_x = _dot(cmb, ht.astype(BF16)) * exp_cum[:, sl]
            st = _dot(bm_t, xdec[:, sl].astype(BF16))
            ht_ref[j] = jnp.exp(cum_last[:, sl]) * ht + st
            y_slabs.append(y_in + y_x + dfull_ref[:, sl] * xs[:, sl])
    y = jnp.concatenate(y_slabs, axis=1)
    y_ref[...] = _gate_norm(y, z_ref[...], nw_ref[...])

    @pl.when(c == pl.num_programs(1) - 1)
    def _():
        for j in range(4):
            hout_ref[0, 2 * j:2 * j + 2] = ht_ref[j].T.reshape(2, HEAD_W, SSD_STATE)
        cout_ref[0] = tail


def _ssd_prompt(z, xbc, dt, cw, cb, dtb, alog, dfull, nw, batch, seq):
    q = SSD_CHUNK
    nc = seq // q
    rowspec = lambda w: pl.BlockSpec((q, w), lambda b, c: (b * nc + c, 0))
    small = lambda a: pl.BlockSpec(a.shape, lambda b, c: (0, 0))
    return pl.pallas_call(
        _ssd_body, grid=(batch, nc),
        in_specs=[rowspec(SSD_W), rowspec(CONV_CH), rowspec(DT_PAD)] + [small(a) for a in (cw, cb, dtb, alog, dfull, nw)],
        out_specs=[rowspec(SSD_W),
                   pl.BlockSpec((1, SSD_HEADS, HEAD_W, SSD_STATE), lambda b, c: (b, 0, 0, 0)),
                   pl.BlockSpec((1, CONV_K - 1, CONV_CH), lambda b, c: (b, 0, 0))],
        out_shape=[jax.ShapeDtypeStruct((batch * seq, SSD_W), F32),
                   jax.ShapeDtypeStruct((batch, SSD_HEADS, HEAD_W, SSD_STATE), F32),
                   jax.ShapeDtypeStruct((batch, CONV_K - 1, CONV_CH), F32)],
        scratch_shapes=[pltpu.VMEM((q + 8, CONV_CH), F32), pltpu.VMEM((4, SSD_STATE, 128), F32)],
        compiler_params=_cparams("parallel", "arbitrary"), name="ssd_prompt")(z, xbc, dt, cw, cb, dtb, alog, dfull, nw)


_SB = 8


def _dec_ssd_body(z_ref, xbc_ref, dt_ref, cs_ref, h_ref, cw_ref, cb_ref, dtb_ref, alog_ref, dfull_ref, nw_ref,
                  *rest):
    y_ref, hout_ref, cout_ref, yrow_ref = rest[-4:]
    x_new = xbc_ref[...]
    c0, c1, c2 = cs_ref[0], cs_ref[1], cs_ref[2]
    conv = cb_ref[...] + cw_ref[0:1, :] * c0 + cw_ref[1:2, :] * c1 + cw_ref[2:3, :] * c2 + cw_ref[3:4, :] * x_new
    cout_ref[0] = c1
    cout_ref[1] = c2
    cout_ref[2] = x_new
    xc = conv * jax.nn.sigmoid(conv)
    xs = xc[:, :SSD_W]
    bmat = xc[:, SSD_W:SSD_W + 2 * SSD_STATE]
    cmat = xc[:, SSD_W + 2 * SSD_STATE:]
    dt = _softplus(dt_ref[...] + dtb_ref[...])
    da = jnp.exp(dt * _a_row(alog_ref))
    emat = _expand_mat()
    xdt = xs * _dot_sel_r(dt, emat)
    da_full = _dot_sel_r(da, emat)
    x_hi, x_lo = _split2(xdt)
    d_hi = da_full.astype(BF16)
    d_r = da_full - d_hi.astype(F32)
    d_mid = d_r.astype(BF16)
    d_lo = (d_r - d_mid.astype(F32)).astype(BF16)
    g0 = lax.broadcasted_iota(jnp.int32, (_SB, SSD_W), 1) < SSD_W // 2
    zx = jnp.zeros((_SB, SSD_W), F32)
    f = lambda a: a.astype(F32)
    x0h, x0l = jnp.where(g0, f(x_hi), 0.0), jnp.where(g0, f(x_lo), 0.0)
    x1h, x1l = jnp.where(g0, 0.0, f(x_hi)), jnp.where(g0, 0.0, f(x_lo))
    xrows = [x0h, x0h, x0l, x1h, x1h, x1l, f(d_hi), f(d_mid), f(d_lo)] + [zx] * 7
    xt = jnp.concatenate(xrows, axis=0).T.astype(BF16)
    b_hi, b_lo = (f(a) for a in _split2(bmat))
    zb = jnp.zeros((_SB, SSD_STATE), F32)
    ob = jnp.ones((_SB, SSD_STATE), F32)
    cat = lambda a, b: jnp.concatenate([a, b], axis=1)
    b0h, b0l, b1h, b1l = b_hi[:, :128], b_lo[:, :128], b_hi[:, 128:], b_lo[:, 128:]
    rrows = [cat(b0h, zb), cat(b0l, zb), cat(b0h, zb), cat(b1h, zb), cat(b1l, zb), cat(b1h, zb),
             cat(zb, ob), cat(zb, ob), cat(zb, ob)] + [cat(zb, zb)] * 7
    r_all = jnp.concatenate(rrows, axis=0)
    rowi = lax.broadcasted_iota(jnp.int32, (16 * _SB, 2 * SSD_STATE), 0) & (_SB - 1)
    row8 = lax.broadcasted_iota(jnp.int32, (8, SSD_STATE), 0)
    lane512 = lax.broadcasted_iota(jnp.int32, (1, SSD_W), 1)
    for i in range(_SB):
        out = _dot(xt, jnp.where(rowi == i, r_all, 0.0).astype(BF16))
        h_new = out[:, SSD_STATE:] * h_ref[i].reshape(SSD_W, SSD_STATE) + out[:, :SSD_STATE]
        hout_ref[i] = h_new.reshape(SSD_HEADS, HEAD_W, SSD_STATE)
        crow = jnp.where(row8 == 0, cmat[i:i + 1, :SSD_STATE],
                         jnp.where(row8 == 1, cmat[i:i + 1, SSD_STATE:], 0.0))
        yy = _dot_nt(crow.astype(BF16), h_new.astype(BF16))
        yrow_ref[i:i + 1, :] = jnp.where(lane512 < SSD_W // 2, yy[0:1, :], yy[1:2, :])
    y = yrow_ref[...] + dfull_ref[...] * xs
    y_ref[...] = _gate_norm(y, z_ref[...], nw_ref[...])


def _ssd_sample(z, xbc, dt, conv_state, ssm_state, layer, cw, cb, dtb, alog, dfull, nw, h_carried):
    nb = z.shape[0]
    assert nb % _SB == 0
    rowspec = lambda w: pl.BlockSpec((_SB, w), lambda i: (i, 0))
    small = lambda a: pl.BlockSpec(a.shape, lambda i: (0, 0))
    cshape, hshape = (CONV_K - 1, _SB, CONV_CH), (_SB, SSD_HEADS, HEAD_W, SSD_STATE)
    cspec = pl.BlockSpec(cshape, lambda i: (0, i, 0))
    cin = pl.BlockSpec((None,) + cshape, lambda i: (layer, 0, i, 0))
    hin = pl.BlockSpec((None,) + hshape, lambda i: (layer, i, 0, 0, 0))
    args = [z, xbc, dt, conv_state, ssm_state, cw, cb, dtb, alog, dfull, nw]
    in_specs = [rowspec(SSD_W), rowspec(CONV_CH), rowspec(DT_PAD), cin, hin] + [small(a) for a in (cw, cb, dtb, alog, dfull, nw)]
    aliases = {}
    if h_carried is not None:
        aliases = {len(args): 1}
        args.append(h_carried)
        in_specs.append(pl.BlockSpec(memory_space=pl.ANY))
    return pl.pallas_call(
        _dec_ssd_body, grid=(nb // _SB,), in_specs=in_specs,
        out_specs=[rowspec(SSD_W), hin, cspec],
        out_shape=[jax.ShapeDtypeStruct((nb, SSD_W), F32),
                   jax.ShapeDtypeStruct(ssm_state.shape, F32),
                   jax.ShapeDtypeStruct(conv_state.shape[1:], F32)],
        input_output_aliases=aliases,
        scratch_shapes=[pltpu.VMEM((_SB, SSD_W), F32)],
        compiler_params=_cparams("parallel"), name="ssd_sample")(*args)


def _row_head_rms(res, gain, scale):
    hl = lax.broadcasted_iota(jnp.int32, (1, ATT_W), 1) >> 6
    sq = res * res
    ms = jnp.zeros_like(res)
    for h in range(4):
        ms_h = jnp.sum(jnp.where(hl == h, sq, 0.0), axis=-1, keepdims=True) * (1.0 / HEAD_W)
        ms = jnp.where(hl == h, ms_h, ms)
    return res * lax.rsqrt(ms + EPS) * gain * scale


def _dec_attn_body(pt_ref, lq1, lk1, lq2, lk2, gnd_ref, gns_ref, qd_ref, kdn_ref, vdn_ref, qs_ref, *rest,
                   n_pages, page, lam_init):
    del pt_ref
    kd, vd = rest[0:n_pages], rest[n_pages:2 * n_pages]
    ks, vs = rest[2 * n_pages:3 * n_pages], rest[3 * n_pages:4 * n_pages]
    od_ref, os_ref = rest[4 * n_pages:]
    lane8 = lax.broadcasted_iota(jnp.int32, (8, ATT_W), 1)
    row8 = lax.broadcasted_iota(jnp.int32, (8, ATT_W), 0)

    qd = qd_ref[0] * (DIFF_QK ** -0.5)
    qm = jnp.where((lane8 >> 5) == row8, qd, 0.0)
    qmb = qm.astype(BF16)
    s_pages = [_dot(qmb, kd[p][...].astype(BF16)) for p in range(n_pages)]
    s_new = jnp.sum(qm * kdn_ref[0], axis=-1, keepdims=True)
    m = s_new
    for s in s_pages:
        m = jnp.maximum(m, jnp.max(s, axis=-1, keepdims=True))
    l = jnp.exp(s_new - m)
    o = l * vdn_ref[0]
    for p in range(n_pages):
        pr = jnp.exp(s_pages[p] - m)
        l = l + jnp.sum(pr, axis=-1, keepdims=True)
        o = o + _dot_nt(pr.astype(BF16), vd[p][...].astype(BF16))
    o = o / l
    lam = _lam(lq1, lk1, lq2, lk2, lam_init)
    coef = jnp.where((lane8 >> 6) == (row8 >> 1), jnp.where((row8 & 1) == 0, 1.0, -lam), 0.0)
    res = jnp.sum(coef * o, axis=0, keepdims=True)
    od_ref[0] = _row_head_rms(res, gnd_ref[...], 1.0 - lam_init)

    qs = qs_ref[0] * (HEAD_W ** -0.5)
    qsm = jnp.where((lane8 >> 6) == row8, qs, 0.0).astype(BF16)
    z = jnp.concatenate([_dot(qsm, ks[p][...].astype(BF16)) for p in range(n_pages)], axis=0)
    r = 8 * n_pages
    sp = _softplus(z)
    log_1m = -sp
    jj = lax.broadcasted_iota(jnp.int32, (page, page), 0)
    ss = lax.broadcasted_iota(jnp.int32, (page, page), 1)
    uincl = jnp.where(jj >= ss, 1.0, 0.0).astype(BF16)
    hi, lo = _split2(log_1m)
    incl = _dot(hi, uincl) + _dot(lo, uincl)
    tot = jnp.broadcast_to(jnp.sum(log_1m, axis=-1, keepdims=True), (r, page))
    ri = lax.broadcasted_iota(jnp.int32, (r, r), 0)
    ci = lax.broadcasted_iota(jnp.int32, (r, r), 1)
    upage = jnp.where(((ri & 7) == (ci & 7)) & ((ci >> 3) > (ri >> 3)), 1.0, 0.0).astype(BF16)
    thi, tlo = _split2(tot)
    carry = _dot(upage, thi) + _dot(upage, tlo)
    a = jnp.exp(z - sp + (incl - log_1m) + carry).astype(BF16)
    osb = None
    for p in range(n_pages):
        d = _dot_nt(a[8 * p:8 * p + 8, :], vs[p][...].astype(BF16))
        osb = d if osb is None else osb + d
    res_s = jnp.sum(jnp.where((lane8 >> 6) == row8, osb, 0.0), axis=0, keepdims=True)
    os_ref[0] = _row_head_rms(res_s, gns_ref[...], 1.0)


def _dec_attn(page_flat, lam_params, gnd, gns, qd, kdn, vdn, qs, caches, layer, n_pages, lam_init):
    nb = qd.shape[0]
    page = caches[0].shape[3]
    assert page == 128
    small = lambda a: pl.BlockSpec(a.shape, lambda b, pt: (0, 0))
    rowspec = pl.BlockSpec((1, 1, ATT_W), lambda b, pt: (b, 0, 0))
    page_specs = []
    for _ in caches:
        for p in range(n_pages):
            page_specs.append(pl.BlockSpec((None, None, ATT_W, page),
                                           lambda b, pt, p=p: (layer, pt[b * n_pages + p], 0, 0)))
    page_args = [c for c in caches for _ in range(n_pages)]
    grid_spec = pltpu.PrefetchScalarGridSpec(
        num_scalar_prefetch=1, grid=(nb,),
        in_specs=[small(a) for a in lam_params] + [small(gnd), small(gns), rowspec, rowspec, rowspec, rowspec] + page_specs,
        out_specs=[rowspec, rowspec])
    return pl.pallas_call(
        functools.partial(_dec_attn_body, n_pages=n_pages, page=page, lam_init=lam_init),
        grid_spec=grid_spec,
        out_shape=[jax.ShapeDtypeStruct((nb, 1, ATT_W), F32)] * 2,
        compiler_params=_cparams("parallel"), name="dec_attn")(
            page_flat, *lam_params, gnd, gns, qd, kdn, vdn, qs, *page_args)


def _split_cols(w):
    outs, i = [], 0
    for n in IN_SIZES:
        outs.append(w[:, i:i + n])
        i += n
    return outs


def _pack_w_in(w):
    dq, dk, dv, z, xbc, dt, sq, sk, sv = _split_cols(w)
    dt = jnp.pad(dt, ((0, 0), (0, DT_PAD - dt.shape[1])))
    w_nat = jnp.concatenate([z, xbc, dt, dk, sk, dq, sq, dv], axis=1).astype(BF16)
    w_t = jnp.concatenate([dk, dv, sk, sv, dq, sq], axis=1).T.astype(BF16)
    return w_nat, w_t


def kernel(x_prompt, x_sample, cache_diff_k, cache_diff_v, cache_sb_k, cache_sb_v, state_ssm, state_conv, page_table, norm_ffn1, w_ffn1_gu, w_ffn1_down, norm_mix, w_in, diff_lambda_q1, diff_lambda_k1, diff_lambda_q2, diff_lambda_k2, diff_norm, ssd_conv_w, ssd_conv_b, ssd_dt_bias, ssd_a_log, ssd_d, ssd_norm, sb_norm, w_out, norm_ffn2, w_ffn2_gu, w_ffn2_down, norm_final):
    batch, seq, _ = x_prompt.shape
    nb = x_sample.shape[0]
    depth = w_in.shape[0]
    n_pages = page_table.shape[1]
    n_pool, page = cache_diff_k.shape[1], cache_diff_k.shape[2]
    tp = batch * seq
    xp = x_prompt.reshape(tp, D_MODEL)
    xs = x_sample.reshape(nb, D_MODEL)
    caches = [jnp.transpose(c, (0, 1, 3, 4, 2)).reshape(depth, n_pool, ATT_W, page)
              for c in (cache_diff_k, cache_diff_v, cache_sb_k, cache_sb_v)]
    conv_in = jnp.transpose(state_conv, (0, 2, 1, 3))
    page_flat = page_table.reshape(-1).astype(jnp.int32)
    row = lambda a: a.reshape(1, -1).astype(F32)
    gfinal = row(norm_final)
    new_p, new_s = [], []
    kv_p = kv_s = h_s = None
    for l in range(depth):
        lam_init = 0.8 - 0.6 * math.exp(-0.3 * l)
        last = l == depth - 1
        wg1, wu1 = w_ffn1_gu[l][:, :D_FF].astype(BF16), w_ffn1_gu[l][:, D_FF:].astype(BF16)
        wd1 = w_ffn1_down[l].astype(BF16)
        wg2, wu2 = w_ffn2_gu[l][:, :D_FF].astype(BF16), w_ffn2_gu[l][:, D_FF:].astype(BF16)
        wd2 = w_ffn2_down[l].astype(BF16)
        w_nat, w_t = _pack_w_in(w_in[l])
        w_out_b = w_out[l].astype(BF16)
        lam_params = [row(a[l]) for a in (diff_lambda_q1, diff_lambda_k1, diff_lambda_q2, diff_lambda_k2)]
        gnd = row(jnp.tile(diff_norm[l], 4))
        gns = row(jnp.tile(sb_norm[l], 4))
        cw, cb = ssd_conv_w[l].astype(F32), row(ssd_conv_b[l])
        dtb = row(jnp.pad(ssd_dt_bias[l], (0, DT_PAD - SSD_HEADS)))
        alog = row(jnp.pad(ssd_a_log[l], (0, DT_PAD - SSD_HEADS)))
        dfull = row(jnp.repeat(ssd_d[l], HEAD_W))
        nw = row(ssd_norm[l])
        g1, gm, g2 = row(norm_ffn1[l]), row(norm_mix[l]), row(norm_ffn2[l])

        xp = _ffn(xp, g1, wg1, wu1, wd1)
        z, xbc, dt, *rest = _inproj(xp, gm, w_nat, w_t, batch, seq, False, l, depth, kv_p)
        kv_p, (dq_t, sq_t, dk_a, sk_a, dv_a, sv_a) = rest[:4], rest[4:]
        od = _diff_attn(dq_t, dk_a, dv_a, lam_params, gnd.reshape(ATT_W, 1), batch, seq, lam_init)
        os_ = _sb_attn(sq_t, sk_a, sv_a, gns.reshape(ATT_W, 1), batch, seq)
        y, h_p, conv_p = _ssd_prompt(z, xbc, dt, cw, cb, dtb, alog, dfull, nw, batch, seq)
        xp = _ffn(xp, g2, wg2, wu2, wd2, gfinal if last else None, mix=(od, y, os_, w_out_b))
        new_p.append((h_p, conv_p))

        xs = _ffn(xs, g1, wg1, wu1, wd1)
        z, xbc, dt, *rest = _inproj(xs, gm, w_nat, w_t, 1, nb, True, l, depth, kv_s)
        kv_s, (dq, sq, dk_n, dv_n) = rest[:4], rest[4:]
        r3 = lambda a: a.reshape(nb, 1, ATT_W)
        od, os_ = _dec_attn(page_flat, lam_params, gnd, gns, r3(dq), r3(dk_n), r3(dv_n), r3(sq), caches, l,
                            n_pages, lam_init)
        y, h_s, conv_s = _ssd_sample(z, xbc, dt, conv_in, state_ssm, l, cw, cb, dtb, alog, dfull, nw, h_s)
        xs = _ffn(xs, g2, wg2, wu2, wd2, gfinal if last else None,
                  mix=(od.reshape(nb, ATT_W), y, os_.reshape(nb, ATT_W), w_out_b))
        new_s.append(conv_s)

    kv_p_out = [jnp.transpose(a.reshape(depth, batch, 4, HEAD_W, seq), (0, 1, 4, 2, 3)) for a in kv_p]
    kv_s_out = [jnp.transpose(a.reshape(depth, 4, HEAD_W, nb), (0, 3, 1, 2)).reshape(depth, nb, 1, 4, HEAD_W)
                for a in kv_s]
    conv_s_out = jnp.transpose(jnp.stack(new_s, axis=0), (0, 2, 1, 3))
    return (xp.reshape(batch, seq, D_MODEL), xs.reshape(nb, 1, D_MODEL), *kv_p_out,
            jnp.stack([p[0] for p in new_p], axis=0), jnp.stack([p[1] for p in new_p], axis=0),
            *kv_s_out, h_s, conv_s_out)
```

```python
import functools
import math

import jax
import jax.numpy as jnp
from jax import lax
from jax.experimental import pallas as pl
from jax.experimental.pallas import tpu as pltpu

F32, BF16 = jnp.float32, jnp.bfloat16

D_MODEL = 1024
D_FF = 2816
DIFF_QK = 32
HEAD_W = 64
ATT_W = 256
SSD_W = 512
SSD_HEADS = 8
SSD_STATE = 128
SSD_CHUNK = 128
CONV_K = 4
CONV_CH = 1024
EPS = 1e-6
NEG = -1e30
IN_SIZES = (256, 256, 256, 512, 1024, 8, 256, 256, 256)
DT_PAD = 128
FF_CHUNK = 256
VMEM_LIMIT = 56 * 1024 * 1024


def _cparams(*sem):
    return pltpu.CompilerParams(dimension_semantics=sem, vmem_limit_bytes=VMEM_LIMIT)


def _rms(x, g):
    return x * lax.rsqrt(jnp.mean(x * x, axis=-1, keepdims=True) + EPS) * g


def _softplus(x):
    return jnp.maximum(x, 0.0) + jnp.log1p(jnp.exp(-jnp.abs(x)))


LOG2E = 1.4426950408889634


def _softplus2(x):
    return jnp.maximum(x, 0.0) + jnp.log2(1.0 + jnp.exp2(-jnp.abs(x)))


def _dot(a, b):
    return jnp.dot(a, b, preferred_element_type=F32)


def _dot_nt(a, b):
    return lax.dot_general(a, b, (((1,), (1,)), ((), ())), preferred_element_type=F32)


def _split2(x):
    hi = x.astype(BF16)
    lo = (x - hi.astype(F32)).astype(BF16)
    return hi, lo


def _split3(x):
    hi = x.astype(BF16)
    r = x - hi.astype(F32)
    mid = r.astype(BF16)
    return hi, mid, (r - mid.astype(F32)).astype(BF16)


def _dot_sel_l(sel, x):
    return sum(_dot(sel, p) for p in _split3(x))


def _dot_sel_r(x, sel):
    return sum(_dot(p, sel) for p in _split3(x))


def _const_spec(shape):
    nd = len(shape)
    return pl.BlockSpec(shape, lambda *_: (0,) * nd, pipeline_mode=pl.Buffered(1))


def _ffn_body(*refs, n_chunks, final, mix):
    refs = list(refs)
    x_ref, g_ref, wg_ref, wu_ref, wd_ref = refs[:5]
    o_ref = refs.pop()
    gf_ref = refs.pop() if final else None
    x = x_ref[...]
    if mix:
        od_ref, y_ref, os_ref, wo_ref = refs[5:9]
        x = x + _dot(od_ref[...].astype(BF16), wo_ref[0:ATT_W, :])
        x = x + _dot(y_ref[...].astype(BF16), wo_ref[ATT_W:ATT_W + SSD_W, :])
        x = x + _dot(os_ref[...].astype(BF16), wo_ref[ATT_W + SSD_W:, :])
    xn = _rms(x, g_ref[...]).astype(BF16)
    acc = jnp.zeros_like(x)
    for c in range(n_chunks):
        sl = slice(c * FF_CHUNK, (c + 1) * FF_CHUNK)
        gate = _dot(xn, wg_ref[:, sl])
        up = _dot(xn, wu_ref[:, sl])
        h = (gate * jax.nn.sigmoid(gate) * up).astype(BF16)
        acc = acc + _dot(h, wd_ref[sl, :])
    y = x + 0.5 * acc
    if final:
        y = _rms(y, gf_ref[...])
    o_ref[...] = y


def _ffn(x, g, wg, wu, wd, final_g=None, mix=None):
    t = x.shape[0]
    tm = min(512, t)
    assert t % tm == 0 and D_FF % FF_CHUNK == 0
    rows = lambda w: pl.BlockSpec((tm, w), lambda i: (i, 0))
    row = rows(D_MODEL)
    in_specs = [row, _const_spec((1, D_MODEL)), _const_spec((D_MODEL, D_FF)), _const_spec((D_MODEL, D_FF)),
                _const_spec((D_FF, D_MODEL))]
    args = [x, g, wg, wu, wd]
    if mix is not None:
        in_specs += [rows(ATT_W), rows(SSD_W), rows(ATT_W), _const_spec((D_MODEL, D_MODEL))]
        args += list(mix)
    if final_g is not None:
        in_specs.append(_const_spec((1, D_MODEL)))
        args.append(final_g)
    return pl.pallas_call(
        functools.partial(_ffn_body, n_chunks=D_FF // FF_CHUNK, final=final_g is not None, mix=mix is not None),
        grid=(t // tm,), in_specs=in_specs, out_specs=row,
        out_shape=jax.ShapeDtypeStruct((t, D_MODEL), F32),
        compiler_params=_cparams("parallel"), name="ffn")(*args)


_NAT_W = (512, 1024, DT_PAD, 256, 256, 256, 256, 256)
_NAT_Z, _NAT_XBC, _NAT_DT, _NAT_DK, _NAT_SK, _NAT_DQ, _NAT_SQ, _NAT_DV = range(8)
_T_DK, _T_DV, _T_SK, _T_SV, _T_DQ, _T_SQ = range(6)
DIFF_TK, SB_TK = 256, 256


def _inproj_body(x_ref, g_ref, wn_ref, wt_ref, *o_refs, sample, tk, n_carried):
    o_refs = o_refs[n_carried:]
    u = _rms(x_ref[...], g_ref[...]).astype(BF16)
    offs = [sum(_NAT_W[:i]) for i in range(len(_NAT_W))]
    nat = lambda i: _dot(u, wn_ref[:, offs[i]:offs[i] + _NAT_W[i]])
    tr = lambda j: _dot_nt(wt_ref[j * ATT_W:(j + 1) * ATT_W, :], u)
    o = list(o_refs)
    for i in (_NAT_Z, _NAT_XBC, _NAT_DT):
        o.pop(0)[...] = nat(i)
    kv_t = [tr(j) for j in (_T_DK, _T_DV, _T_SK, _T_SV)]
    for v in kv_t:
        o.pop(0)[...] = v
    if sample:
        for i in (_NAT_DQ, _NAT_SQ, _NAT_DK, _NAT_DV):
            o.pop(0)[...] = nat(i)
    else:
        o.pop(0)[...] = tr(_T_DQ)
        o.pop(0)[...] = tr(_T_SQ)
        o.pop(0)[...] = nat(_NAT_DK).astype(BF16)
        o.pop(0)[...] = nat(_NAT_SK).astype(BF16)
        for v, tkv in zip((kv_t[1], kv_t[3]), tk):
            r = o.pop(0)
            for c in range(v.shape[1] // tkv):
                r[c] = v[:, c * tkv:(c + 1) * tkv].astype(BF16)
    assert not o


def _inproj(x, g, w_nat, w_t, batch, seq, sample, layer, depth, kv_carried):
    t = x.shape[0]
    tm = min(512, seq)
    tk = (min(DIFF_TK, seq), min(SB_TK, seq))
    assert seq % tm == 0 and all(tm % t_ == 0 for t_ in tk)
    nj = seq // tm
    rows = lambda w: pl.BlockSpec((tm, w), lambda i: (i, 0))
    tspec = pl.BlockSpec((None, ATT_W, tm), lambda i: (i // nj, 0, i % nj))
    tshape = jax.ShapeDtypeStruct((batch, ATT_W, seq), F32)
    kvspec = pl.BlockSpec((None, None, ATT_W, tm), lambda i: (layer, i // nj, 0, i % nj))
    kvshape = jax.ShapeDtypeStruct((depth, batch, ATT_W, seq), F32)
    out_specs = [rows(_NAT_W[i]) for i in (_NAT_Z, _NAT_XBC, _NAT_DT)] + [kvspec] * 4
    out_shape = [jax.ShapeDtypeStruct((t, _NAT_W[i]), F32) for i in (_NAT_Z, _NAT_XBC, _NAT_DT)] + [kvshape] * 4
    carried = list(kv_carried) if kv_carried is not None else []
    n_fixed = 4
    aliases = {n_fixed + i: 3 + i for i in range(len(carried))}
    if sample:
        out_specs += [rows(ATT_W)] * 4
        out_shape += [jax.ShapeDtypeStruct((t, ATT_W), F32)] * 4
    else:
        out_specs += [tspec] * 2 + [rows(ATT_W)] * 2
        out_shape += [tshape] * 2 + [jax.ShapeDtypeStruct((t, ATT_W), BF16)] * 2
        out_specs += [pl.BlockSpec((None, tm // t_, ATT_W, t_), lambda i: (i // nj, i % nj, 0, 0)) for t_ in tk]
        out_shape += [jax.ShapeDtypeStruct((batch, seq // t_, ATT_W, t_), BF16) for t_ in tk]
    return pl.pallas_call(
        functools.partial(_inproj_body, sample=sample, tk=tk, n_carried=len(carried)), grid=(t // tm,),
        in_specs=[rows(D_MODEL), _const_spec((1, D_MODEL)), _const_spec(w_nat.shape), _const_spec(w_t.shape)]
        + [pl.BlockSpec(memory_space=pl.ANY)] * len(carried),
        out_specs=out_specs, out_shape=out_shape, input_output_aliases=aliases,
        compiler_params=_cparams("parallel"), name="inproj")(x, g, w_nat, w_t, *carried)


def _lam(lq1, lk1, lq2, lk2, lam_init):
    s1 = jnp.sum(lq1[...] * lk1[...], axis=-1, keepdims=True)
    s2 = jnp.sum(lq2[...] * lk2[...], axis=-1, keepdims=True)
    return jnp.exp(s1) - jnp.exp(s2) + lam_init


def _attn_specs(gain_args, batch, seq, tq):
    nq = seq // tq
    small = lambda a: pl.BlockSpec(a.shape, lambda b, i: (0, 0))
    in_specs = [small(a) for a in gain_args] + [
        pl.BlockSpec((None, ATT_W, tq), lambda b, i: (b, 0, i)),
        pl.BlockSpec((seq, ATT_W), lambda b, i: (b, 0)),
        pl.BlockSpec((None, nq, ATT_W, tq), lambda b, i: (b, 0, 0, 0))]
    out_spec = pl.BlockSpec((tq, ATT_W), lambda b, i: (b * nq + i, 0))
    return in_specs, out_spec


def _norm_heads_t(o_t, gcol_ref, scale):
    parts = []
    for h in range(4):
        rows = slice(h * HEAD_W, (h + 1) * HEAD_W)
        oh = o_t[rows, :]
        ms = jnp.mean(oh * oh, axis=0, keepdims=True)
        parts.append(oh * lax.rsqrt(ms + EPS) * (gcol_ref[rows, :] * scale))
    return jnp.concatenate(parts, axis=0).T


def _diff_attn_body(lq1, lk1, lq2, lk2, gcol_ref, q_ref, k_ref, v_ref, o_ref,
                    qs_ref, m_ref, l_ref, acc_ref, sa_ref, sb_ref, *, tq, lam_init):
    qi = pl.program_id(1)
    q = q_ref[...] * (DIFF_QK ** -0.5 * LOG2E)
    seg = lax.broadcasted_iota(jnp.int32, (ATT_W, tq), 0) >> 5
    for j in range(8):
        qs_ref[j] = jnp.where(seg == j, q, 0.0).astype(BF16)
    m_ref[...] = jnp.full((8, tq), NEG, F32)
    l_ref[...] = jnp.zeros((8, tq), F32)
    acc_ref[...] = jnp.zeros((2, ATT_W, tq), F32)
    ones_rows = jnp.ones((16, tq), BF16)

    def scores(kj, s_ref):
        start = pl.multiple_of(kj * tq, tq)
        kblk = k_ref[pl.ds(start, tq), :]
        for j in range(8):
            s_ref[j] = _dot(kblk, qs_ref[j])

    def softmax_pv(kj, s_ref, masked):
        if masked:
            keep = (lax.broadcasted_iota(jnp.int32, (tq, tq), 0) <= lax.broadcasted_iota(jnp.int32, (tq, tq), 1))
        m_all, l_all = m_ref[...], l_ref[...]
        m_rows, l_rows = [], []
        for j in range(8):
            h, mi = j // 2, j % 2
            rows = slice(h * HEAD_W, (h + 1) * HEAD_W)
            s = s_ref[j]
            if masked:
                s = jnp.where(keep, s, NEG)
            m_prev = m_all[j:j + 1, :]
            m_new = jnp.maximum(m_prev, jnp.max(s, axis=0, keepdims=True))
            alpha = jnp.exp2(m_prev - m_new)
            p = jnp.exp2(s - m_new)
            pv = _dot(jnp.concatenate([v_ref[kj, rows, :], ones_rows], axis=0), p.astype(BF16))
            l_rows.append(alpha * l_all[j:j + 1, :] + pv[HEAD_W:HEAD_W + 1, :])
            m_rows.append(m_new)
            acc_ref[mi, rows, :] = acc_ref[mi, rows, :] * alpha + pv[:HEAD_W, :]
        m_ref[...] = jnp.concatenate(m_rows, axis=0)
        l_ref[...] = jnp.concatenate(l_rows, axis=0)

    scores(0, sa_ref)

    def loop_body(i, carry):
        scores(2 * i + 1, sb_ref)
        softmax_pv(2 * i, sa_ref, False)
        scores(jnp.minimum(2 * i + 2, qi), sa_ref)
        softmax_pv(2 * i + 1, sb_ref, False)
        return carry

    lax.fori_loop(0, qi >> 1, loop_body, 0)

    @pl.when((qi & 1) == 1)
    def _():
        scores(qi, sb_ref)
        softmax_pv(qi - 1, sa_ref, False)
        softmax_pv(qi, sb_ref, True)

    @pl.when((qi & 1) == 0)
    def _():
        softmax_pv(qi, sa_ref, True)

    lam = _lam(lq1, lk1, lq2, lk2, lam_init)
    linv = 1.0 / l_ref[...]
    parts = []
    for h in range(4):
        rows = slice(h * HEAD_W, (h + 1) * HEAD_W)
        parts.append(acc_ref[0, rows, :] * linv[2 * h:2 * h + 1, :]
                     - lam * (acc_ref[1, rows, :] * linv[2 * h + 1:2 * h + 2, :]))
    o_ref[...] = _norm_heads_t(jnp.concatenate(parts, axis=0), gcol_ref, 1.0 - lam_init)


def _diff_attn(q_t, k, v_t, lam_params, gcol, batch, seq, lam_init):
    tq = min(DIFF_TK, seq)
    assert seq % tq == 0
    in_specs, out_spec = _attn_specs(list(lam_params) + [gcol], batch, seq, tq)
    return pl.pallas_call(
        functools.partial(_diff_attn_body, tq=tq, lam_init=lam_init),
        grid=(batch, seq // tq), in_specs=in_specs, out_specs=out_spec,
        out_shape=jax.ShapeDtypeStruct((batch * seq, ATT_W), F32),
        scratch_shapes=[pltpu.VMEM((8, ATT_W, tq), BF16), pltpu.VMEM((8, tq), F32),
                        pltpu.VMEM((8, tq), F32), pltpu.VMEM((2, ATT_W, tq), F32),
                        pltpu.VMEM((8, tq, tq), F32), pltpu.VMEM((8, tq, tq), F32)],
        compiler_params=_cparams("parallel", "arbitrary"), name="diff_attn")(*lam_params, gcol, q_t, k, v_t)


def _sb_attn_body(gcol_ref, q_ref, k_ref, v_ref, o_ref, qs_ref, c_ref, acc_ref, sa_ref, sb_ref, *, tq):
    qi = pl.program_id(1)
    q = q_ref[...] * (HEAD_W ** -0.5 * LOG2E)
    hrow = lax.broadcasted_iota(jnp.int32, (ATT_W, tq), 0) >> 6
    for h in range(4):
        qs_ref[h] = jnp.where(hrow == h, q, 0.0).astype(BF16)
    c_ref[...] = jnp.zeros((8, tq), F32)
    acc_ref[...] = jnp.zeros((ATT_W, tq), F32)
    ki = lax.broadcasted_iota(jnp.int32, (tq, tq), 0)
    kl = lax.broadcasted_iota(jnp.int32, (tq, tq), 1)
    later = jnp.where(kl >= ki, 1.0, 0.0).astype(BF16)
    later2 = jnp.concatenate([later, later], axis=1)

    def zscores(kj, z_ref):
        start = pl.multiple_of(kj * tq, tq)
        kblk = k_ref[pl.ds(start, tq), :]
        for h in range(4):
            z_ref[h] = _dot(kblk, qs_ref[h])

    def weigh_pv(kj, z_ref, masked):
        if masked:
            keep = ki < kl
        zs = [z_ref[h] for h in range(4)]
        incls = []
        for h in range(4):
            sp = _softplus2(zs[h])
            if masked:
                sp = jnp.where(keep, sp, 0.0)
            hi, lo = _split2(sp)
            incls.append(_dot(later2, jnp.concatenate([hi, lo], axis=0)))
        c_all = c_ref[...]
        c_rows = []
        for h in range(4):
            rows = slice(h * HEAD_W, (h + 1) * HEAD_W)
            c = c_all[h:h + 1, :]
            a = jnp.exp2(zs[h] - (c + incls[h]))
            if masked:
                a = jnp.where(keep, a, 0.0)
            c_rows.append(c + incls[h][0:1, :])
            acc_ref[rows, :] = acc_ref[rows, :] + _dot(v_ref[kj, rows, :], a.astype(BF16))
        c_ref[...] = jnp.concatenate(c_rows + [c_all[4:8, :]], axis=0)

    zscores(qi, sa_ref)
    zscores(jnp.maximum(qi - 1, 0), sb_ref)
    weigh_pv(qi, sa_ref, True)

    def loop_body(i, carry):
        u = qi - 1 - 2 * i
        zscores(u - 1, sa_ref)
        weigh_pv(u, sb_ref, False)
        zscores(jnp.maximum(u - 2, 0), sb_ref)
        weigh_pv(u - 1, sa_ref, False)
        return carry

    lax.fori_loop(0, qi >> 1, loop_body, 0)

    @pl.when((qi & 1) == 1)
    def _():
        weigh_pv(0, sb_ref, False)

    o_ref[...] = _norm_heads_t(acc_ref[...], gcol_ref, 1.0)


def _sb_attn(q_t, k, v_t, gcol, batch, seq):
    tq = min(SB_TK, seq)
    assert seq % tq == 0
    in_specs, out_spec = _attn_specs([gcol], batch, seq, tq)
    return pl.pallas_call(
        functools.partial(_sb_attn_body, tq=tq),
        grid=(batch, seq // tq), in_specs=in_specs, out_specs=out_spec,
        out_shape=jax.ShapeDtypeStruct((batch * seq, ATT_W), F32),
        scratch_shapes=[pltpu.VMEM((4, ATT_W, tq), BF16), pltpu.VMEM((8, tq), F32),
                        pltpu.VMEM((ATT_W, tq), F32),
                        pltpu.VMEM((4, tq, tq), F32), pltpu.VMEM((4, tq, tq), F32)],
        compiler_params=_cparams("parallel", "arbitrary"), name="sb_attn")(gcol, q_t, k, v_t)


def _expand_mat():
    r = lax.broadcasted_iota(jnp.int32, (DT_PAD, SSD_W), 0)
    c = lax.broadcasted_iota(jnp.int32, (DT_PAD, SSD_W), 1) >> 6
    return jnp.where(r == c, 1.0, 0.0).astype(BF16)


def _a_row(alog_ref):
    lane = lax.broadcasted_iota(jnp.int32, (1, DT_PAD), 1)
    return jnp.where(lane < SSD_HEADS, -jnp.exp(alog_ref[...]), 0.0)


def _gate_norm(y, z, nw):
    y = y * (z * jax.nn.sigmoid(z))
    half = SSD_W // 2
    parts = [_rms(y[:, g * half:(g + 1) * half], nw[:, g * half:(g + 1) * half]) for g in range(2)]
    return jnp.concatenate(parts, axis=1)


def _ssd_body(z_ref, xbc_ref, dt_ref, cw_ref, cb_ref, dtb_ref, alog_ref, dfull_ref, nw_ref,
              y_ref, hout_ref, cout_ref, ext_ref, ht_ref):
    c = pl.program_id(1)
    q = SSD_CHUNK

    @pl.when(c == 0)
    def _():
        ext_ref[0:8, :] = jnp.zeros((8, CONV_CH), F32)
        ht_ref[...] = jnp.zeros_like(ht_ref)

    ext_ref[8:8 + q, :] = xbc_ref[...]
    conv = cb_ref[...]
    for k in range(CONV_K):
        conv = conv + cw_ref[k:k + 1, :] * ext_ref[5 + k:5 + k + q, :]
    tail = ext_ref[q + 5:q + 8, :]
    ext_ref[5:8, :] = tail
    xc = conv * jax.nn.sigmoid(conv)
    xs = xc[:, :SSD_W]
    dt = _softplus(dt_ref[...] + dtb_ref[...])
    dta = dt * _a_row(alog_ref)
    ti = lax.broadcasted_iota(jnp.int32, (q, q), 0)
    si = lax.broadcasted_iota(jnp.int32, (q, q), 1)
    causal = ti >= si
    cum = _dot_sel_l(jnp.where(causal, 1.0, 0.0).astype(BF16), dta)
    emat = _expand_mat()
    cum_full = _dot_sel_r(cum, emat)
    xdt = xs * _dot_sel_r(dt, emat)
    cum_last = cum_full[q - 1:q, :]
    xdec = xdt * jnp.exp(cum_last - cum_full)
    exp_cum = jnp.exp(cum_full)
    cum_t = cum.T
    lane = lax.broadcasted_iota(jnp.int32, (1, 128), 1)
    y_slabs = []
    for g in range(2):
        bm = xc[:, SSD_W + g * SSD_STATE:SSD_W + (g + 1) * SSD_STATE]
        cm = xc[:, SSD_W + 2 * SSD_STATE + g * SSD_STATE:SSD_W + 2 * SSD_STATE + (g + 1) * SSD_STATE]
        bmb, cmb = bm.astype(BF16), cm.astype(BF16)
        cb = _dot_nt(cmb, bmb)
        bm_t = bm.T.astype(BF16)
        for j in (2 * g, 2 * g + 1):
            sl = slice(j * 128, (j + 1) * 128)
            xdt_s = xdt[:, sl]
            y_in = None
            for hh in range(2):
                h = 2 * j + hh
                segd = cum[:, h:h + 1] - cum_t[h:h + 1, :]
                w = (cb * jnp.exp(jnp.where(causal, segd, NEG))).astype(BF16)
                xh = jnp.where((lane >> 6) == hh, xdt_s, 0.0).astype(BF16)
                d = _dot(w, xh)
                y_in = d if y_in is None else y_in + d
            ht = ht_ref[j]
            y_x = _dot(cmb, ht.astype(BF16)) * exp_cum[:, sl]
            st = _dot(bm_t, xdec[:, sl].astype(BF16))
            ht_ref[j] = jnp.exp(cum_last[:, sl]) * ht + st
            y_slabs.append(y_in + y_x + dfull_ref[:, sl] * xs[:, sl])
    y = jnp.concatenate(y_slabs, axis=1)
    y_ref[...] = _gate_norm(y, z_ref[...], nw_ref[...])

    @pl.when(c == pl.num_programs(1) - 1)
    def _():
        for j in range(4):
            hout_ref[0, 2 * j:2 * j + 2] = ht_ref[j].T.reshape(2, HEAD_W, SSD_STATE)
        cout_ref[0] = tail


def _ssd_prompt(z, xbc, dt, cw, cb, dtb, alog, dfull, nw, batch, seq):
    q = SSD_CHUNK
    nc = seq // q
    rowspec = lambda w: pl.BlockSpec((q, w), lambda b, c: (b * nc + c, 0))
    small = lambda a: pl.BlockSpec(a.shape, lambda b, c: (0, 0))
    return pl.pallas_call(
        _ssd_body, grid=(batch, nc),
        in_specs=[rowspec(SSD_W), rowspec(CONV_CH), rowspec(DT_PAD)] + [small(a) for a in (cw, cb, dtb, alog, dfull, nw)],
        out_specs=[rowspec(SSD_W),
                   pl.BlockSpec((1, SSD_HEADS, HEAD_W, SSD_STATE), lambda b, c: (b, 0, 0, 0)),
                   pl.BlockSpec((1, CONV_K - 1, CONV_CH), lambda b, c: (b, 0, 0))],
        out_shape=[jax.ShapeDtypeStruct((batch * seq, SSD_W), F32),
                   jax.ShapeDtypeStruct((batch, SSD_HEADS, HEAD_W, SSD_STATE), F32),
                   jax.ShapeDtypeStruct((batch, CONV_K - 1, CONV_CH), F32)],
        scratch_shapes=[pltpu.VMEM((q + 8, CONV_CH), F32), pltpu.VMEM((4, SSD_STATE, 128), F32)],
        compiler_params=_cparams("parallel", "arbitrary"), name="ssd_prompt")(z, xbc, dt, cw, cb, dtb, alog, dfull, nw)


_SB = 8


def _dec_ssd_body(z_ref, xbc_ref, dt_ref, cs_ref, h_ref, cw_ref, cb_ref, dtb_ref, alog_ref, dfull_ref, nw_ref,
                  *rest):
    y_ref, hout_ref, cout_ref, yrow_ref = rest[-4:]
    x_new = xbc_ref[...]
    c0, c1, c2 = cs_ref[0], cs_ref[1], cs_ref[2]
    conv = cb_ref[...] + cw_ref[0:1, :] * c0 + cw_ref[1:2, :] * c1 + cw_ref[2:3, :] * c2 + cw_ref[3:4, :] * x_new
    cout_ref[0] = c1
    cout_ref[1] = c2
    cout_ref[2] = x_new
    xc = conv * jax.nn.sigmoid(conv)
    xs = xc[:, :SSD_W]
    bmat = xc[:, SSD_W:SSD_W + 2 * SSD_STATE]
    cmat = xc[:, SSD_W + 2 * SSD_STATE:]
    dt = _softplus(dt_ref[...] + dtb_ref[...])
    da = jnp.exp(dt * _a_row(alog_ref))
    emat = _expand_mat()
    xdt = xs * _dot_sel_r(dt, emat)
    da_full = _dot_sel_r(da, emat)
    x_hi, x_lo = _split2(xdt)
    d_hi = da_full.astype(BF16)
    d_r = da_full - d_hi.astype(F32)
    d_mid = d_r.astype(BF16)
    d_lo = (d_r - d_mid.astype(F32)).astype(BF16)
    g0 = lax.broadcasted_iota(jnp.int32, (_SB, SSD_W), 1) < SSD_W // 2
    zx = jnp.zeros((_SB, SSD_W), F32)
    f = lambda a: a.astype(F32)
    x0h, x0l = jnp.where(g0, f(x_hi), 0.0), jnp.where(g0, f(x_lo), 0.0)
    x1h, x1l = jnp.where(g0, 0.0, f(x_hi)), jnp.where(g0, 0.0, f(x_lo))
    xrows = [x0h, x0h, x0l, x1h, x1h, x1l, f(d_hi), f(d_mid), f(d_lo)] + [zx] * 7
    xt = jnp.concatenate(xrows, axis=0).T.astype(BF16)
    b_hi, b_lo = (f(a) for a in _split2(bmat))
    zb = jnp.zeros((_SB, SSD_STATE), F32)
    ob = jnp.ones((_SB, SSD_STATE), F32)
    cat = lambda a, b: jnp.concatenate([a, b], axis=1)
    b0h, b0l, b1h, b1l = b_hi[:, :128], b_lo[:, :128], b_hi[:, 128:], b_lo[:, 128:]
    rrows = [cat(b0h, zb), cat(b0l, zb), cat(b0h, zb), cat(b1h, zb), cat(b1l, zb), cat(b1h, zb),
             cat(zb, ob), cat(zb, ob), cat(zb, ob)] + [cat(zb, zb)] * 7
    r_all = jnp.concatenate(rrows, axis=0)
    rowi = lax.broadcasted_iota(jnp.int32, (16 * _SB, 2 * SSD_STATE), 0) & (_SB - 1)
    row8 = lax.broadcasted_iota(jnp.int32, (8, SSD_STATE), 0)
    lane512 = lax.broadcasted_iota(jnp.int32, (1, SSD_W), 1)
    for i in range(_SB):
        out = _dot(xt, jnp.where(rowi == i, r_all, 0.0).astype(BF16))
        h_new = out[:, SSD_STATE:] * h_ref[i].reshape(SSD_W, SSD_STATE) + out[:, :SSD_STATE]
        hout_ref[i] = h_new.reshape(SSD_HEADS, HEAD_W, SSD_STATE)
        crow = jnp.where(row8 == 0, cmat[i:i + 1, :SSD_STATE],
                         jnp.where(row8 == 1, cmat[i:i + 1, SSD_STATE:], 0.0))
        yy = _dot_nt(crow.astype(BF16), h_new.astype(BF16))
        yrow_ref[i:i + 1, :] = jnp.where(lane512 < SSD_W // 2, yy[0:1, :], yy[1:2, :])
    y = yrow_ref[...] + dfull_ref[...] * xs
    y_ref[...] = _gate_norm(y, z_ref[...], nw_ref[...])


def _ssd_sample(z, xbc, dt, conv_state, ssm_state, layer, cw, cb, dtb, alog, dfull, nw, h_carried):
    nb = z.shape[0]
    assert nb % _SB == 0
    rowspec = lambda w: pl.BlockSpec((_SB, w), lambda i: (i, 0))
    small = lambda a: pl.BlockSpec(a.shape, lambda i: (0, 0))
    cshape, hshape = (CONV_K - 1, _SB, CONV_CH), (_SB, SSD_HEADS, HEAD_W, SSD_STATE)
    cspec = pl.BlockSpec(cshape, lambda i: (0, i, 0))
    cin = pl.BlockSpec((None,) + cshape, lambda i: (layer, 0, i, 0))
    hin = pl.BlockSpec((None,) + hshape, lambda i: (layer, i, 0, 0, 0))
    args = [z, xbc, dt, conv_state, ssm_state, cw, cb, dtb, alog, dfull, nw]
    in_specs = [rowspec(SSD_W), rowspec(CONV_CH), rowspec(DT_PAD), cin, hin] + [small(a) for a in (cw, cb, dtb, alog, dfull, nw)]
    aliases = {}
    if h_carried is not None:
        aliases = {len(args): 1}
        args.append(h_carried)
        in_specs.append(pl.BlockSpec(memory_space=pl.ANY))
    return pl.pallas_call(
        _dec_ssd_body, grid=(nb // _SB,), in_specs=in_specs,
        out_specs=[rowspec(SSD_W), hin, cspec],
        out_shape=[jax.ShapeDtypeStruct((nb, SSD_W), F32),
                   jax.ShapeDtypeStruct(ssm_state.shape, F32),
                   jax.ShapeDtypeStruct(conv_state.shape[1:], F32)],
        input_output_aliases=aliases,
        scratch_shapes=[pltpu.VMEM((_SB, SSD_W), F32)],
        compiler_params=_cparams("parallel"), name="ssd_sample")(*args)


def _row_head_rms(res, gain, scale):
    hl = lax.broadcasted_iota(jnp.int32, (1, ATT_W), 1) >> 6
    sq = res * res
    ms = jnp.zeros_like(res)
    for h in range(4):
        ms_h = jnp.sum(jnp.where(hl == h, sq, 0.0), axis=-1, keepdims=True) * (1.0 / HEAD_W)
        ms = jnp.where(hl == h, ms_h, ms)
    return res * lax.rsqrt(ms + EPS) * gain * scale


def _dec_attn_one(lam, gnd_ref, gns_ref, qd_row, kdn_row, vdn_row, qs_row, pages, n_pages, page, lam_init):
    kd, vd, ks, vs = ([functools.partial(pages, c, p) for p in range(n_pages)] for c in range(4))
    lane8 = lax.broadcasted_iota(jnp.int32, (8, ATT_W), 1)
    row8 = lax.broadcasted_iota(jnp.int32, (8, ATT_W), 0)

    qd = qd_row * (DIFF_QK ** -0.5)
    qm = jnp.where((lane8 >> 5) == row8, qd, 0.0)
    qmb = qm.astype(BF16)
    s_pages = [_dot(qmb, kd[p]().astype(BF16)) for p in range(n_pages)]
    s_new = jnp.sum(qm * kdn_row, axis=-1, keepdims=True)
    m = s_new
    for s in s_pages:
        m = jnp.maximum(m, jnp.max(s, axis=-1, keepdims=True))
    l = jnp.exp(s_new - m)
    o = l * vdn_row
    for p in range(n_pages):
        pr = jnp.exp(s_pages[p] - m)
        l = l + jnp.sum(pr, axis=-1, keepdims=True)
        o = o + _dot_nt(pr.astype(BF16), vd[p]().astype(BF16))
    o = o / l
    coef = jnp.where((lane8 >> 6) == (row8 >> 1), jnp.where((row8 & 1) == 0, 1.0, -lam), 0.0)
    res = jnp.sum(coef * o, axis=0, keepdims=True)
    od = _row_head_rms(res, gnd_ref[...], 1.0 - lam_init)

    qs = qs_row * (HEAD_W ** -0.5)
    qsm = jnp.where((lane8 >> 6) == row8, qs, 0.0).astype(BF16)
    z = jnp.concatenate([_dot(qsm, ks[p]().astype(BF16)) for p in range(n_pages)], axis=0)
    r = 8 * n_pages
    sp = _softplus(z)
    log_1m = -sp
    jj = lax.broadcasted_iota(jnp.int32, (page, page), 0)
    ss = lax.broadcasted_iota(jnp.int32, (page, page), 1)
    uincl = jnp.where(jj >= ss, 1.0, 0.0).astype(BF16)
    hi, lo = _split2(log_1m)
    incl = _dot(hi, uincl) + _dot(lo, uincl)
    tot = jnp.broadcast_to(jnp.sum(log_1m, axis=-1, keepdims=True), (r, page))
    ri = lax.broadcasted_iota(jnp.int32, (r, r), 0)
    ci = lax.broadcasted_iota(jnp.int32, (r, r), 1)
    upage = jnp.where(((ri & 7) == (ci & 7)) & ((ci >> 3) > (ri >> 3)), 1.0, 0.0).astype(BF16)
    thi, tlo = _split2(tot)
    carry = _dot(upage, thi) + _dot(upage, tlo)
    a = jnp.exp(z - sp + (incl - log_1m) + carry).astype(BF16)
    osb = None
    for p in range(n_pages):
        d = _dot_nt(a[8 * p:8 * p + 8, :], vs[p]().astype(BF16))
        osb = d if osb is None else osb + d
    res_s = jnp.sum(jnp.where((lane8 >> 6) == row8, osb, 0.0), axis=0, keepdims=True)
    return od, _row_head_rms(res_s, gns_ref[...], 1.0)


def _dec_attn_body(pt_ref, lq1, lk1, lq2, lk2, gnd_ref, gns_ref, qd_ref, kdn_ref, vdn_ref, qs_ref,
                   c0_ref, c1_ref, c2_ref, c3_ref, od_ref, os_ref, buf_ref, sem_ref,
                   *, layer, n_pages, page, lam_init, nb):
    i = pl.program_id(0)
    cache_refs = (c0_ref, c1_ref, c2_ref, c3_ref)

    def page_copies(seq, slot):
        return [pltpu.make_async_copy(cache_refs[c].at[layer, pt_ref[seq * n_pages + p]],
                                      buf_ref.at[slot, c, p], sem_ref.at[slot])
                for c in range(4) for p in range(n_pages)]

    def start(seq, slot):
        for cp in page_copies(seq, slot):
            cp.start()

    def wait(seq, slot):
        for cp in page_copies(seq, slot):
            cp.wait()

    lam = _lam(lq1, lk1, lq2, lk2, lam_init)

    def attend(r, slot):
        od, os_ = _dec_attn_one(lam, gnd_ref, gns_ref, qd_ref[r], kdn_ref[r], vdn_ref[r], qs_ref[r],
                                lambda c, p: buf_ref[slot, c, p], n_pages, page, lam_init)
        od_ref[r] = od
        os_ref[r] = os_

    @pl.when(i == 0)
    def _():
        start(0, 0)

    start(2 * i + 1, 1)
    wait(2 * i, 0)
    attend(0, 0)

    @pl.when(2 * i + 2 < nb)
    def _():
        start(2 * i + 2, 0)

    wait(2 * i + 1, 1)
    attend(1, 1)


def _dec_attn(page_flat, lam_params, gnd, gns, qd, kdn, vdn, qs, caches, layer, n_pages, lam_init):
    nb = qd.shape[0]
    page = caches[0].shape[3]
    assert page == 128 and nb % 2 == 0
    small = lambda a: pl.BlockSpec(a.shape, lambda b, pt: (0, 0))
    rowspec = pl.BlockSpec((2, 1, ATT_W), lambda b, pt: (b, 0, 0))
    grid_spec = pltpu.PrefetchScalarGridSpec(
        num_scalar_prefetch=1, grid=(nb // 2,),
        in_specs=[small(a) for a in lam_params] + [small(gnd), small(gns), rowspec, rowspec, rowspec, rowspec]
        + [pl.BlockSpec(memory_space=pl.ANY)] * 4,
        out_specs=[rowspec, rowspec],
        scratch_shapes=[pltpu.VMEM((2, 4, n_pages, ATT_W, page), F32), pltpu.SemaphoreType.DMA((2,))])
    return pl.pallas_call(
        functools.partial(_dec_attn_body, layer=layer, n_pages=n_pages, page=page, lam_init=lam_init, nb=nb),
        grid_spec=grid_spec,
        out_shape=[jax.ShapeDtypeStruct((nb, 1, ATT_W), F32)] * 2,
        compiler_params=_cparams("arbitrary"), name="dec_attn")(
            page_flat, *lam_params, gnd, gns, qd, kdn, vdn, qs, *caches)


def _split_cols(w):
    outs, i = [], 0
    for n in IN_SIZES:
        outs.append(w[:, i:i + n])
        i += n
    return outs


def _pack_w_in(w):
    dq, dk, dv, z, xbc, dt, sq, sk, sv = _split_cols(w)
    dt = jnp.pad(dt, ((0, 0), (0, DT_PAD - dt.shape[1])))
    w_nat = jnp.concatenate([z, xbc, dt, dk, sk, dq, sq, dv], axis=1).astype(BF16)
    w_t = jnp.concatenate([dk, dv, sk, sv, dq, sq], axis=1).T.astype(BF16)
    return w_nat, w_t


def kernel(x_prompt, x_sample, cache_diff_k, cache_diff_v, cache_sb_k, cache_sb_v, state_ssm, state_conv, page_table, norm_ffn1, w_ffn1_gu, w_ffn1_down, norm_mix, w_in, diff_lambda_q1, diff_lambda_k1, diff_lambda_q2, diff_lambda_k2, diff_norm, ssd_conv_w, ssd_conv_b, ssd_dt_bias, ssd_a_log, ssd_d, ssd_norm, sb_norm, w_out, norm_ffn2, w_ffn2_gu, w_ffn2_down, norm_final):
    batch, seq, _ = x_prompt.shape
    nb = x_sample.shape[0]
    depth = w_in.shape[0]
    n_pages = page_table.shape[1]
    n_pool, page = cache_diff_k.shape[1], cache_diff_k.shape[2]
    tp = batch * seq
    xp = x_prompt.reshape(tp, D_MODEL)
    xs = x_sample.reshape(nb, D_MODEL)
    caches = [jnp.transpose(c, (0, 1, 3, 4, 2)).reshape(depth, n_pool, ATT_W, page)
              for c in (cache_diff_k, cache_diff_v, cache_sb_k, cache_sb_v)]
    conv_in = jnp.transpose(state_conv, (0, 2, 1, 3))
    page_flat = page_table.reshape(-1).astype(jnp.int32)
    row = lambda a: a.reshape(1, -1).astype(F32)
    gfinal = row(norm_final)
    new_p, new_s = [], []
    kv_p = kv_s = h_s = None
    for l in range(depth):
        lam_init = 0.8 - 0.6 * math.exp(-0.3 * l)
        last = l == depth - 1
        wg1, wu1 = w_ffn1_gu[l][:, :D_FF].astype(BF16), w_ffn1_gu[l][:, D_FF:].astype(BF16)
        wd1 = w_ffn1_down[l].astype(BF16)
        wg2, wu2 = w_ffn2_gu[l][:, :D_FF].astype(BF16), w_ffn2_gu[l][:, D_FF:].astype(BF16)
        wd2 = w_ffn2_down[l].astype(BF16)
        w_nat, w_t = _pack_w_in(w_in[l])
        w_out_b = w_out[l].astype(BF16)
        lam_params = [row(a[l]) for a in (diff_lambda_q1, diff_lambda_k1, diff_lambda_q2, diff_lambda_k2)]
        gnd = row(jnp.tile(diff_norm[l], 4))
        gns = row(jnp.tile(sb_norm[l], 4))
        cw, cb = ssd_conv_w[l].astype(F32), row(ssd_conv_b[l])
        dtb = row(jnp.pad(ssd_dt_bias[l], (0, DT_PAD - SSD_HEADS)))
        alog = row(jnp.pad(ssd_a_log[l], (0, DT_PAD - SSD_HEADS)))
        dfull = row(jnp.repeat(ssd_d[l], HEAD_W))
        nw = row(ssd_norm[l])
        g1, gm, g2 = row(norm_ffn1[l]), row(norm_mix[l]), row(norm_ffn2[l])

        xp = _ffn(xp, g1, wg1, wu1, wd1)
        z, xbc, dt, *rest = _inproj(xp, gm, w_nat, w_t, batch, seq, False, l, depth, kv_p)
        kv_p, (dq_t, sq_t, dk_a, sk_a, dv_a, sv_a) = rest[:4], rest[4:]
        od = _diff_attn(dq_t, dk_a, dv_a, lam_params, gnd.reshape(ATT_W, 1), batch, seq, lam_init)
        os_ = _sb_attn(sq_t, sk_a, sv_a, gns.reshape(ATT_W, 1), batch, seq)
        y, h_p, conv_p = _ssd_prompt(z, xbc, dt, cw, cb, dtb, alog, dfull, nw, batch, seq)
        xp = _ffn(xp, g2, wg2, wu2, wd2, gfinal if last else None, mix=(od, y, os_, w_out_b))
        new_p.append((h_p, conv_p))

        xs = _ffn(xs, g1, wg1, wu1, wd1)
        z, xbc, dt, *rest = _inproj(xs, gm, w_nat, w_t, 1, nb, True, l, depth, kv_s)
        kv_s, (dq, sq, dk_n, dv_n) = rest[:4], rest[4:]
        r3 = lambda a: a.reshape(nb, 1, ATT_W)
        od, os_ = _dec_attn(page_flat, lam_params, gnd, gns, r3(dq), r3(dk_n), r3(dv_n), r3(sq), caches, l,
                            n_pages, lam_init)
        y, h_s, conv_s = _ssd_sample(z, xbc, dt, conv_in, state_ssm, l, cw, cb, dtb, alog, dfull, nw, h_s)
        xs = _ffn(xs, g2, wg2, wu2, wd2, gfinal if last else None,
                  mix=(od.reshape(nb, ATT_W), y, os_.reshape(nb, ATT_W), w_out_b))
        new_s.append(conv_s)

    kv_p_out = [jnp.transpose(a.reshape(depth, batch, 4, HEAD_W, seq), (0, 1, 4, 2, 3)) for a in kv_p]
    kv_s_out = [jnp.transpose(a.reshape(depth, 4, HEAD_W, nb), (0, 3, 1, 2)).reshape(depth, nb, 1, 4, HEAD_W)
                for a in kv_s]
    conv_s_out = jnp.transpose(jnp.stack(new_s, axis=0), (0, 2, 1, 3))
    return (xp.reshape(batch, seq, D_MODEL), xs.reshape(nb, 1, D_MODEL), *kv_p_out,
            jnp.stack([p[0] for p in new_p], axis=0), jnp.stack([p[1] for p in new_p], axis=0),
            *kv_s_out, h_s, conv_s_out)
```

```python
import functools
import math

import jax
import jax.numpy as jnp
from jax import lax
from jax.experimental import pallas as pl
from jax.experimental.pallas import tpu as pltpu

F32, BF16 = jnp.float32, jnp.bfloat16

D_MODEL = 1024
D_FF = 2816
DIFF_QK = 32
HEAD_W = 64
ATT_W = 256
SSD_W = 512
SSD_HEADS = 8
SSD_STATE = 128
SSD_CHUNK = 128
CONV_K = 4
CONV_CH = 1024
EPS = 1e-6
NEG = -1e30
IN_SIZES = (256, 256, 256, 512, 1024, 8, 256, 256, 256)
DT_PAD = 128
FF_CHUNK = 256
VMEM_LIMIT = 56 * 1024 * 1024


def _cparams(*sem):
    return pltpu.CompilerParams(dimension_semantics=sem, vmem_limit_bytes=VMEM_LIMIT)


def _rms(x, g):
    return x * lax.rsqrt(jnp.mean(x * x, axis=-1, keepdims=True) + EPS) * g


def _softplus(x):
    return jnp.maximum(x, 0.0) + jnp.log1p(jnp.exp(-jnp.abs(x)))


LOG2E = 1.4426950408889634


def _softplus2(x):
    return jnp.maximum(x, 0.0) + jnp.log2(1.0 + jnp.exp2(-jnp.abs(x)))


def _dot(a, b):
    return jnp.dot(a, b, preferred_element_type=F32)


def _dot_nt(a, b):
    return lax.dot_general(a, b, (((1,), (1,)), ((), ())), preferred_element_type=F32)


def _split2(x):
    hi = x.astype(BF16)
    lo = (x - hi.astype(F32)).astype(BF16)
    return hi, lo


def _split3(x):
    hi = x.astype(BF16)
    r = x - hi.astype(F32)
    mid = r.astype(BF16)
    return hi, mid, (r - mid.astype(F32)).astype(BF16)


def _dot_sel_l(sel, x):
    return sum(_dot(sel, p) for p in _split3(x))


def _dot_sel_r(x, sel):
    return sum(_dot(p, sel) for p in _split3(x))


def _const_spec(shape):
    nd = len(shape)
    return pl.BlockSpec(shape, lambda *_: (0,) * nd, pipeline_mode=pl.Buffered(1))


def _ffn_body(*refs, n_chunks, final, mix):
    refs = list(refs)
    x_ref, g_ref, wg_ref, wu_ref, wd_ref = refs[:5]
    o_ref = refs.pop()
    gf_ref = refs.pop() if final else None
    x = x_ref[...]
    if mix:
        od_ref, y_ref, os_ref, wo_ref = refs[5:9]
        x = x + _dot(od_ref[...].astype(BF16), wo_ref[0:ATT_W, :])
        x = x + _dot(y_ref[...].astype(BF16), wo_ref[ATT_W:ATT_W + SSD_W, :])
        x = x + _dot(os_ref[...].astype(BF16), wo_ref[ATT_W + SSD_W:, :])
    xn = _rms(x, g_ref[...]).astype(BF16)
    acc = jnp.zeros_like(x)
    for c in range(n_chunks):
        sl = slice(c * FF_CHUNK, (c + 1) * FF_CHUNK)
        gate = _dot(xn, wg_ref[:, sl])
        up = _dot(xn, wu_ref[:, sl])
        h = (gate * jax.nn.sigmoid(gate) * up).astype(BF16)
        acc = acc + _dot(h, wd_ref[sl, :])
    y = x + 0.5 * acc
    if final:
        y = _rms(y, gf_ref[...])
    o_ref[...] = y


def _ffn(x, g, wg, wu, wd, final_g=None, mix=None):
    t = x.shape[0]
    tm = min(512, t)
    assert t % tm == 0 and D_FF % FF_CHUNK == 0
    rows = lambda w: pl.BlockSpec((tm, w), lambda i: (i, 0))
    row = rows(D_MODEL)
    in_specs = [row, _const_spec((1, D_MODEL)), _const_spec((D_MODEL, D_FF)), _const_spec((D_MODEL, D_FF)),
                _const_spec((D_FF, D_MODEL))]
    args = [x, g, wg, wu, wd]
    if mix is not None:
        in_specs += [rows(ATT_W), rows(SSD_W), rows(ATT_W), _const_spec((D_MODEL, D_MODEL))]
        args += list(mix)
    if final_g is not None:
        in_specs.append(_const_spec((1, D_MODEL)))
        args.append(final_g)
    return pl.pallas_call(
        functools.partial(_ffn_body, n_chunks=D_FF // FF_CHUNK, final=final_g is not None, mix=mix is not None),
        grid=(t // tm,), in_specs=in_specs, out_specs=row,
        out_shape=jax.ShapeDtypeStruct((t, D_MODEL), F32),
        compiler_params=_cparams("parallel"), name="ffn")(*args)


_NAT_W = (512, 1024, DT_PAD, 256, 256, 256, 256, 256)
_NAT_Z, _NAT_XBC, _NAT_DT, _NAT_DK, _NAT_SK, _NAT_DQ, _NAT_SQ, _NAT_DV = range(8)
_T_DK, _T_DV, _T_SK, _T_SV, _T_DQ, _T_SQ = range(6)
DIFF_TK, SB_TK = 256, 256


def _inproj_body(x_ref, g_ref, wn_ref, wt_ref, *o_refs, sample, tk, n_carried):
    o_refs = o_refs[n_carried:]
    u = _rms(x_ref[...], g_ref[...]).astype(BF16)
    offs = [sum(_NAT_W[:i]) for i in range(len(_NAT_W))]
    nat = lambda i: _dot(u, wn_ref[:, offs[i]:offs[i] + _NAT_W[i]])
    tr = lambda j: _dot_nt(wt_ref[j * ATT_W:(j + 1) * ATT_W, :], u)
    o = list(o_refs)
    for i in (_NAT_Z, _NAT_XBC, _NAT_DT):
        o.pop(0)[...] = nat(i)
    kv_t = [tr(j) for j in (_T_DK, _T_DV, _T_SK, _T_SV)]
    for v in kv_t:
        o.pop(0)[...] = v
    if sample:
        for i in (_NAT_DQ, _NAT_SQ, _NAT_DK, _NAT_DV):
            o.pop(0)[...] = nat(i)
    else:
        o.pop(0)[...] = tr(_T_DQ)
        o.pop(0)[...] = tr(_T_SQ)
        o.pop(0)[...] = nat(_NAT_DK).astype(BF16)
        o.pop(0)[...] = nat(_NAT_SK).astype(BF16)
        for v, tkv in zip((kv_t[1], kv_t[3]), tk):
            r = o.pop(0)
            for c in range(v.shape[1] // tkv):
                r[c] = v[:, c * tkv:(c + 1) * tkv].astype(BF16)
    assert not o


def _inproj(x, g, w_nat, w_t, batch, seq, sample, layer, depth, kv_carried):
    t = x.shape[0]
    tm = min(512, seq)
    tk = (min(DIFF_TK, seq), min(SB_TK, seq))
    assert seq % tm == 0 and all(tm % t_ == 0 for t_ in tk)
    nj = seq // tm
    rows = lambda w: pl.BlockSpec((tm, w), lambda i: (i, 0))
    tspec = pl.BlockSpec((None, ATT_W, tm), lambda i: (i // nj, 0, i % nj))
    tshape = jax.ShapeDtypeStruct((batch, ATT_W, seq), F32)
    kvspec = pl.BlockSpec((None, None, ATT_W, tm), lambda i: (layer, i // nj, 0, i % nj))
    kvshape = jax.ShapeDtypeStruct((depth, batch, ATT_W, seq), F32)
    out_specs = [rows(_NAT_W[i]) for i in (_NAT_Z, _NAT_XBC, _NAT_DT)] + [kvspec] * 4
    out_shape = [jax.ShapeDtypeStruct((t, _NAT_W[i]), F32) for i in (_NAT_Z, _NAT_XBC, _NAT_DT)] + [kvshape] * 4
    carried = list(kv_carried) if kv_carried is not None else []
    n_fixed = 4
    aliases = {n_fixed + i: 3 + i for i in range(len(carried))}
    if sample:
        out_specs += [rows(ATT_W)] * 4
        out_shape += [jax.ShapeDtypeStruct((t, ATT_W), F32)] * 4
    else:
        out_specs += [tspec] * 2 + [rows(ATT_W)] * 2
        out_shape += [tshape] * 2 + [jax.ShapeDtypeStruct((t, ATT_W), BF16)] * 2
        out_specs += [pl.BlockSpec((None, tm // t_, ATT_W, t_), lambda i: (i // nj, i % nj, 0, 0)) for t_ in tk]
        out_shape += [jax.ShapeDtypeStruct((batch, seq // t_, ATT_W, t_), BF16) for t_ in tk]
    return pl.pallas_call(
        functools.partial(_inproj_body, sample=sample, tk=tk, n_carried=len(carried)), grid=(t // tm,),
        in_specs=[rows(D_MODEL), _const_spec((1, D_MODEL)), _const_spec(w_nat.shape), _const_spec(w_t.shape)]
        + [pl.BlockSpec(memory_space=pl.ANY)] * len(carried),
        out_specs=out_specs, out_shape=out_shape, input_output_aliases=aliases,
        compiler_params=_cparams("parallel"), name="inproj")(x, g, w_nat, w_t, *carried)


def _lam(lq1, lk1, lq2, lk2, lam_init):
    s1 = jnp.sum(lq1[...] * lk1[...], axis=-1, keepdims=True)
    s2 = jnp.sum(lq2[...] * lk2[...], axis=-1, keepdims=True)
    return jnp.exp(s1) - jnp.exp(s2) + lam_init


def _attn_specs(gain_args, batch, seq, tq):
    nq = seq // tq
    small = lambda a: pl.BlockSpec(a.shape, lambda b, i: (0, 0))
    in_specs = [small(a) for a in gain_args] + [
        pl.BlockSpec((None, ATT_W, tq), lambda b, i: (b, 0, i)),
        pl.BlockSpec((seq, ATT_W), lambda b, i: (b, 0)),
        pl.BlockSpec((None, nq, ATT_W, tq), lambda b, i: (b, 0, 0, 0))]
    out_spec = pl.BlockSpec((tq, ATT_W), lambda b, i: (b * nq + i, 0))
    return in_specs, out_spec


def _norm_heads_t(o_t, gcol_ref, scale):
    parts = []
    for h in range(4):
        rows = slice(h * HEAD_W, (h + 1) * HEAD_W)
        oh = o_t[rows, :]
        ms = jnp.mean(oh * oh, axis=0, keepdims=True)
        parts.append(oh * lax.rsqrt(ms + EPS) * (gcol_ref[rows, :] * scale))
    return jnp.concatenate(parts, axis=0).T


def _diff_attn_body(lq1, lk1, lq2, lk2, gcol_ref, q_ref, k_ref, v_ref, o_ref,
                    qs_ref, m_ref, l_ref, acc_ref, sa_ref, sb_ref, *, tq, lam_init):
    qi = pl.program_id(1)
    q = q_ref[...] * (DIFF_QK ** -0.5 * LOG2E)
    seg = lax.broadcasted_iota(jnp.int32, (ATT_W, tq), 0) >> 5
    for j in range(8):
        qs_ref[j] = jnp.where(seg == j, q, 0.0).astype(BF16)
    m_ref[...] = jnp.full((8, tq), NEG, F32)
    l_ref[...] = jnp.zeros((8, tq), F32)
    acc_ref[...] = jnp.zeros((2, ATT_W, tq), F32)
    ones_rows = jnp.ones((16, tq), BF16)

    def scores(kj, s_ref):
        start = pl.multiple_of(kj * tq, tq)
        kblk = k_ref[pl.ds(start, tq), :]
        for j in range(8):
            s_ref[j] = _dot(kblk, qs_ref[j])

    def softmax_pv(kj, s_ref, masked):
        if masked:
            keep = (lax.broadcasted_iota(jnp.int32, (tq, tq), 0) <= lax.broadcasted_iota(jnp.int32, (tq, tq), 1))
        m_all, l_all = m_ref[...], l_ref[...]
        m_rows, l_rows = [], []
        for j in range(8):
            h, mi = j // 2, j % 2
            rows = slice(h * HEAD_W, (h + 1) * HEAD_W)
            s = s_ref[j]
            if masked:
                s = jnp.where(keep, s, NEG)
            m_prev = m_all[j:j + 1, :]
            m_new = jnp.maximum(m_prev, jnp.max(s, axis=0, keepdims=True))
            alpha = jnp.exp2(m_prev - m_new)
            p = jnp.exp2(s - m_new)
            pv = _dot(jnp.concatenate([v_ref[kj, rows, :], ones_rows], axis=0), p.astype(BF16))
            l_rows.append(alpha * l_all[j:j + 1, :] + pv[HEAD_W:HEAD_W + 1, :])
            m_rows.append(m_new)
            acc_ref[mi, rows, :] = acc_ref[mi, rows, :] * alpha + pv[:HEAD_W, :]
        m_ref[...] = jnp.concatenate(m_rows, axis=0)
        l_ref[...] = jnp.concatenate(l_rows, axis=0)

    scores(0, sa_ref)

    def loop_body(i, carry):
        scores(2 * i + 1, sb_ref)
        softmax_pv(2 * i, sa_ref, False)
        scores(jnp.minimum(2 * i + 2, qi), sa_ref)
        softmax_pv(2 * i + 1, sb_ref, False)
        return carry

    lax.fori_loop(0, qi >> 1, loop_body, 0)

    @pl.when((qi & 1) == 1)
    def _():
        scores(qi, sb_ref)
        softmax_pv(qi - 1, sa_ref, False)
        softmax_pv(qi, sb_ref, True)

    @pl.when((qi & 1) == 0)
    def _():
        softmax_pv(qi, sa_ref, True)

    lam = _lam(lq1, lk1, lq2, lk2, lam_init)
    linv = 1.0 / l_ref[...]
    parts = []
    for h in range(4):
        rows = slice(h * HEAD_W, (h + 1) * HEAD_W)
        parts.append(acc_ref[0, rows, :] * linv[2 * h:2 * h + 1, :]
                     - lam * (acc_ref[1, rows, :] * linv[2 * h + 1:2 * h + 2, :]))
    o_ref[...] = _norm_heads_t(jnp.concatenate(parts, axis=0), gcol_ref, 1.0 - lam_init)


def _diff_attn(q_t, k, v_t, lam_params, gcol, batch, seq, lam_init):
    tq = min(DIFF_TK, seq)
    assert seq % tq == 0
    in_specs, out_spec = _attn_specs(list(lam_params) + [gcol], batch, seq, tq)
    return pl.pallas_call(
        functools.partial(_diff_attn_body, tq=tq, lam_init=lam_init),
        grid=(batch, seq // tq), in_specs=in_specs, out_specs=out_spec,
        out_shape=jax.ShapeDtypeStruct((batch * seq, ATT_W), F32),
        scratch_shapes=[pltpu.VMEM((8, ATT_W, tq), BF16), pltpu.VMEM((8, tq), F32),
                        pltpu.VMEM((8, tq), F32), pltpu.VMEM((2, ATT_W, tq), F32),
                        pltpu.VMEM((8, tq, tq), F32), pltpu.VMEM((8, tq, tq), F32)],
        compiler_params=_cparams("parallel", "arbitrary"), name="diff_attn")(*lam_params, gcol, q_t, k, v_t)


def _sb_attn_body(gcol_ref, q_ref, k_ref, v_ref, o_ref, qs_ref, c_ref, acc_ref, sa_ref, sb_ref, *, tq):
    qi = pl.program_id(1)
    q = q_ref[...] * (HEAD_W ** -0.5 * LOG2E)
    hrow = lax.broadcasted_iota(jnp.int32, (ATT_W, tq), 0) >> 6
    for h in range(4):
        qs_ref[h] = jnp.where(hrow == h, q, 0.0).astype(BF16)
    c_ref[...] = jnp.zeros((8, tq), F32)
    acc_ref[...] = jnp.zeros((ATT_W, tq), F32)
    ki = lax.broadcasted_iota(jnp.int32, (tq, tq), 0)
    kl = lax.broadcasted_iota(jnp.int32, (tq, tq), 1)
    later = jnp.where(kl >= ki, 1.0, 0.0).astype(BF16)
    later2 = jnp.concatenate([later, later], axis=1)

    def zscores(kj, z_ref):
        start = pl.multiple_of(kj * tq, tq)
        kblk = k_ref[pl.ds(start, tq), :]
        for h in range(4):
            z_ref[h] = _dot(kblk, qs_ref[h])

    def weigh_pv(kj, z_ref, masked):
        if masked:
            keep = ki < kl
        zs = [z_ref[h] for h in range(4)]
        incls = []
        for h in range(4):
            sp = _softplus2(zs[h])
            if masked:
                sp = jnp.where(keep, sp, 0.0)
            hi, lo = _split2(sp)
            incls.append(_dot(later2, jnp.concatenate([hi, lo], axis=0)))
        c_all = c_ref[...]
        c_rows = []
        for h in range(4):
            rows = slice(h * HEAD_W, (h + 1) * HEAD_W)
            c = c_all[h:h + 1, :]
            a = jnp.exp2(zs[h] - (c + incls[h]))
            if masked:
                a = jnp.where(keep, a, 0.0)
            c_rows.append(c + incls[h][0:1, :])
            acc_ref[rows, :] = acc_ref[rows, :] + _dot(v_ref[kj, rows, :], a.astype(BF16))
        c_ref[...] = jnp.concatenate(c_rows + [c_all[4:8, :]], axis=0)

    zscores(qi, sa_ref)
    zscores(jnp.maximum(qi - 1, 0), sb_ref)
    weigh_pv(qi, sa_ref, True)

    def loop_body(i, carry):
        u = qi - 1 - 2 * i
        zscores(u - 1, sa_ref)
        weigh_pv(u, sb_ref, False)
        zscores(jnp.maximum(u - 2, 0), sb_ref)
        weigh_pv(u - 1, sa_ref, False)
        return carry

    lax.fori_loop(0, qi >> 1, loop_body, 0)

    @pl.when((qi & 1) == 1)
    def _():
        weigh_pv(0, sb_ref, False)

    o_ref[...] = _norm_heads_t(acc_ref[...], gcol_ref, 1.0)


def _sb_attn(q_t, k, v_t, gcol, batch, seq):
    tq = min(SB_TK, seq)
    assert seq % tq == 0
    in_specs, out_spec = _attn_specs([gcol], batch, seq, tq)
    return pl.pallas_call(
        functools.partial(_sb_attn_body, tq=tq),
        grid=(batch, seq // tq), in_specs=in_specs, out_specs=out_spec,
        out_shape=jax.ShapeDtypeStruct((batch * seq, ATT_W), F32),
        scratch_shapes=[pltpu.VMEM((4, ATT_W, tq), BF16), pltpu.VMEM((8, tq), F32),
                        pltpu.VMEM((ATT_W, tq), F32),
                        pltpu.VMEM((4, tq, tq), F32), pltpu.VMEM((4, tq, tq), F32)],
        compiler_params=_cparams("parallel", "arbitrary"), name="sb_attn")(gcol, q_t, k, v_t)


def _expand_mat():
    r = lax.broadcasted_iota(jnp.int32, (DT_PAD, SSD_W), 0)
    c = lax.broadcasted_iota(jnp.int32, (DT_PAD, SSD_W), 1) >> 6
    return jnp.where(r == c, 1.0, 0.0).astype(BF16)


def _a_row(alog_ref):
    lane = lax.broadcasted_iota(jnp.int32, (1, DT_PAD), 1)
    return jnp.where(lane < SSD_HEADS, -jnp.exp(alog_ref[...]), 0.0)


def _gate_norm(y, z, nw):
    y = y * (z * jax.nn.sigmoid(z))
    half = SSD_W // 2
    parts = [_rms(y[:, g * half:(g + 1) * half], nw[:, g * half:(g + 1) * half]) for g in range(2)]
    return jnp.concatenate(parts, axis=1)


def _ssd_body(z_ref, xbc_ref, dt_ref, cw_ref, cb_ref, dtb_ref, alog_ref, dfull_ref, nw_ref,
              y_ref, hout_ref, cout_ref, ext_ref, ht_ref):
    c = pl.program_id(1)
    q = SSD_CHUNK

    @pl.when(c == 0)
    def _():
        ext_ref[0:8, :] = jnp.zeros((8, CONV_CH), F32)
        ht_ref[...] = jnp.zeros_like(ht_ref)

    ext_ref[8:8 + q, :] = xbc_ref[...]
    conv = cb_ref[...]
    for k in range(CONV_K):
        conv = conv + cw_ref[k:k + 1, :] * ext_ref[5 + k:5 + k + q, :]
    tail = ext_ref[q + 5:q + 8, :]
    ext_ref[5:8, :] = tail
    xc = conv * jax.nn.sigmoid(conv)
    xs = xc[:, :SSD_W]
    dt = _softplus(dt_ref[...] + dtb_ref[...])
    dta = dt * _a_row(alog_ref)
    ti = lax.broadcasted_iota(jnp.int32, (q, q), 0)
    si = lax.broadcasted_iota(jnp.int32, (q, q), 1)
    causal = ti >= si
    cum = _dot_sel_l(jnp.where(causal, 1.0, 0.0).astype(BF16), dta)
    emat = _expand_mat()
    cum_full = _dot_sel_r(cum, emat)
    xdt = xs * _dot_sel_r(dt, emat)
    cum_last = cum_full[q - 1:q, :]
    xdec = xdt * jnp.exp(cum_last - cum_full)
    exp_cum = jnp.exp(cum_full)
    cum_t = cum.T
    lane = lax.broadcasted_iota(jnp.int32, (1, 128), 1)
    y_slabs = []
    for g in range(2):
        bm = xc[:, SSD_W + g * SSD_STATE:SSD_W + (g + 1) * SSD_STATE]
        cm = xc[:, SSD_W + 2 * SSD_STATE + g * SSD_STATE:SSD_W + 2 * SSD_STATE + (g + 1) * SSD_STATE]
        bmb, cmb = bm.astype(BF16), cm.astype(BF16)
        cb = _dot_nt(cmb, bmb)
        bm_t = bm.T.astype(BF16)
        for j in (2 * g, 2 * g + 1):
            sl = slice(j * 128, (j + 1) * 128)
            xdt_s = xdt[:, sl]
            y_in = None
            for hh in range(2):
                h = 2 * j + hh
                segd = cum[:, h:h + 1] - cum_t[h:h + 1, :]
                w = (cb * jnp.exp(jnp.where(causal, segd, NEG))).astype(BF16)
                xh = jnp.where((lane >> 6) == hh, xdt_s, 0.0).astype(BF16)
                d = _dot(w, xh)
                y_in = d if y_in is None else y_in + d
            ht = ht_ref[j]
            y_x = _dot(cmb, ht.astype(BF16)) * exp_cum[:, sl]
            st = _dot(bm_t, xdec[:, sl].astype(BF16))
            ht_ref[j] = jnp.exp(cum_last[:, sl]) * ht + st
            y_slabs.append(y_in + y_x + dfull_ref[:, sl] * xs[:, sl])
    y = jnp.concatenate(y_slabs, axis=1)
    y_ref[...] = _gate_norm(y, z_ref[...], nw_ref[...])

    @pl.when(c == pl.num_programs(1) - 1)
    def _():
        for j in range(4):
            hout_ref[0, 2 * j:2 * j + 2] = ht_ref[j].T.reshape(2, HEAD_W, SSD_STATE)
        cout_ref[0] = tail


def _ssd_prompt(z, xbc, dt, cw, cb, dtb, alog, dfull, nw, batch, seq):
    q = SSD_CHUNK
    nc = seq // q
    rowspec = lambda w: pl.BlockSpec((q, w), lambda b, c: (b * nc + c, 0))
    small = lambda a: pl.BlockSpec(a.shape, lambda b, c: (0, 0))
    return pl.pallas_call(
        _ssd_body, grid=(batch, nc),
        in_specs=[rowspec(SSD_W), rowspec(CONV_CH), rowspec(DT_PAD)] + [small(a) for a in (cw, cb, dtb, alog, dfull, nw)],
        out_specs=[rowspec(SSD_W),
                   pl.BlockSpec((1, SSD_HEADS, HEAD_W, SSD_STATE), lambda b, c: (b, 0, 0, 0)),
                   pl.BlockSpec((1, CONV_K - 1, CONV_CH), lambda b, c: (b, 0, 0))],
        out_shape=[jax.ShapeDtypeStruct((batch * seq, SSD_W), F32),
                   jax.ShapeDtypeStruct((batch, SSD_HEADS, HEAD_W, SSD_STATE), F32),
                   jax.ShapeDtypeStruct((batch, CONV_K - 1, CONV_CH), F32)],
        scratch_shapes=[pltpu.VMEM((q + 8, CONV_CH), F32), pltpu.VMEM((4, SSD_STATE, 128), F32)],
        compiler_params=_cparams("parallel", "arbitrary"), name="ssd_prompt")(z, xbc, dt, cw, cb, dtb, alog, dfull, nw)


_SB = 8


def _dec_ssd_body(z_ref, xbc_ref, dt_ref, cs_ref, h_ref, cw_ref, cb_ref, dtb_ref, alog_ref, dfull_ref, nw_ref,
                  *rest):
    y_ref, hout_ref, cout_ref, yrow_ref = rest[-4:]
    x_new = xbc_ref[...]
    c0, c1, c2 = cs_ref[0], cs_ref[1], cs_ref[2]
    conv = cb_ref[...] + cw_ref[0:1, :] * c0 + cw_ref[1:2, :] * c1 + cw_ref[2:3, :] * c2 + cw_ref[3:4, :] * x_new
    cout_ref[0] = c1
    cout_ref[1] = c2
    cout_ref[2] = x_new
    xc = conv * jax.nn.sigmoid(conv)
    xs = xc[:, :SSD_W]
    bmat = xc[:, SSD_W:SSD_W + 2 * SSD_STATE]
    cmat = xc[:, SSD_W + 2 * SSD_STATE:]
    dt = _softplus(dt_ref[...] + dtb_ref[...])
    da = jnp.exp(dt * _a_row(alog_ref))
    emat = _expand_mat()
    xdt = xs * _dot_sel_r(dt, emat)
    da_full = _dot_sel_r(da, emat)
    x_hi, x_lo = _split2(xdt)
    d_hi = da_full.astype(BF16)
    d_r = da_full - d_hi.astype(F32)
    d_mid = d_r.astype(BF16)
    d_lo = (d_r - d_mid.astype(F32)).astype(BF16)
    g0 = lax.broadcasted_iota(jnp.int32, (_SB, SSD_W), 1) < SSD_W // 2
    zx = jnp.zeros((_SB, SSD_W), F32)
    f = lambda a: a.astype(F32)
    x0h, x0l = jnp.where(g0, f(x_hi), 0.0), jnp.where(g0, f(x_lo), 0.0)
    x1h, x1l = jnp.where(g0, 0.0, f(x_hi)), jnp.where(g0, 0.0, f(x_lo))
    xrows = [x0h, x0h, x0l, x1h, x1h, x1l, f(d_hi), f(d_mid), f(d_lo)] + [zx] * 7
    xt = jnp.concatenate(xrows, axis=0).T.astype(BF16)
    b_hi, b_lo = (f(a) for a in _split2(bmat))
    zb = jnp.zeros((_SB, SSD_STATE), F32)
    ob = jnp.ones((_SB, SSD_STATE), F32)
    cat = lambda a, b: jnp.concatenate([a, b], axis=1)
    b0h, b0l, b1h, b1l = b_hi[:, :128], b_lo[:, :128], b_hi[:, 128:], b_lo[:, 128:]
    rrows = [cat(b0h, zb), cat(b0l, zb), cat(b0h, zb), cat(b1h, zb), cat(b1l, zb), cat(b1h, zb),
             cat(zb, ob), cat(zb, ob), cat(zb, ob)] + [cat(zb, zb)] * 7
    r_all = jnp.concatenate(rrows, axis=0)
    rowi = lax.broadcasted_iota(jnp.int32, (16 * _SB, 2 * SSD_STATE), 0) & (_SB - 1)
    row8 = lax.broadcasted_iota(jnp.int32, (8, SSD_STATE), 0)
    lane512 = lax.broadcasted_iota(jnp.int32, (1, SSD_W), 1)
    for i in range(_SB):
        out = _dot(xt, jnp.where(rowi == i, r_all, 0.0).astype(BF16))
        h_new = out[:, SSD_STATE:] * h_ref[i].reshape(SSD_W, SSD_STATE) + out[:, :SSD_STATE]
        hout_ref[i] = h_new.reshape(SSD_HEADS, HEAD_W, SSD_STATE)
        crow = jnp.where(row8 == 0, cmat[i:i + 1, :SSD_STATE],
                         jnp.where(row8 == 1, cmat[i:i + 1, SSD_STATE:], 0.0))
        yy = _dot_nt(crow.astype(BF16), h_new.astype(BF16))
        yrow_ref[i:i + 1, :] = jnp.where(lane512 < SSD_W // 2, yy[0:1, :], yy[1:2, :])
    y = yrow_ref[...] + dfull_ref[...] * xs
    y_ref[...] = _gate_norm(y, z_ref[...], nw_ref[...])


def _ssd_sample(z, xbc, dt, conv_state, ssm_state, layer, cw, cb, dtb, alog, dfull, nw, h_carried):
    nb = z.shape[0]
    assert nb % _SB == 0
    rowspec = lambda w: pl.BlockSpec((_SB, w), lambda i: (i, 0))
    small = lambda a: pl.BlockSpec(a.shape, lambda i: (0, 0))
    cshape, hshape = (CONV_K - 1, _SB, CONV_CH), (_SB, SSD_HEADS, HEAD_W, SSD_STATE)
    cspec = pl.BlockSpec(cshape, lambda i: (0, i, 0))
    cin = pl.BlockSpec((None,) + cshape, lambda i: (layer, 0, i, 0))
    hin = pl.BlockSpec((None,) + hshape, lambda i: (layer, i, 0, 0, 0))
    args = [z, xbc, dt, conv_state, ssm_state, cw, cb, dtb, alog, dfull, nw]
    in_specs = [rowspec(SSD_W), rowspec(CONV_CH), rowspec(DT_PAD), cin, hin] + [small(a) for a in (cw, cb, dtb, alog, dfull, nw)]
    aliases = {}
    if h_carried is not None:
        aliases = {len(args): 1}
        args.append(h_carried)
        in_specs.append(pl.BlockSpec(memory_space=pl.ANY))
    return pl.pallas_call(
        _dec_ssd_body, grid=(nb // _SB,), in_specs=in_specs,
        out_specs=[rowspec(SSD_W), hin, cspec],
        out_shape=[jax.ShapeDtypeStruct((nb, SSD_W), F32),
                   jax.ShapeDtypeStruct(ssm_state.shape, F32),
                   jax.ShapeDtypeStruct(conv_state.shape[1:], F32)],
        input_output_aliases=aliases,
        scratch_shapes=[pltpu.VMEM((_SB, SSD_W), F32)],
        compiler_params=_cparams("parallel"), name="ssd_sample")(*args)


def _row_head_rms(res, gain, scale):
    hl = lax.broadcasted_iota(jnp.int32, (1, ATT_W), 1) >> 6
    sq = res * res
    ms = jnp.zeros_like(res)
    for h in range(4):
        ms_h = jnp.sum(jnp.where(hl == h, sq, 0.0), axis=-1, keepdims=True) * (1.0 / HEAD_W)
        ms = jnp.where(hl == h, ms_h, ms)
    return res * lax.rsqrt(ms + EPS) * gain * scale


def _dec_attn_one(lam, gnd_ref, gns_ref, qd_row, kdn_row, vdn_row, qs_row, pages, n_pages, page, lam_init):
    kd, vd, ks, vs = ([functools.partial(pages, c, p) for p in range(n_pages)] for c in range(4))
    lane8 = lax.broadcasted_iota(jnp.int32, (8, ATT_W), 1)
    row8 = lax.broadcasted_iota(jnp.int32, (8, ATT_W), 0)

    qd = qd_row * (DIFF_QK ** -0.5)
    qm = jnp.where((lane8 >> 5) == row8, qd, 0.0)
    qmb = qm.astype(BF16)
    s_pages = [_dot(qmb, kd[p]().astype(BF16)) for p in range(n_pages)]
    s_new = jnp.sum(qm * kdn_row, axis=-1, keepdims=True)
    m = s_new
    for s in s_pages:
        m = jnp.maximum(m, jnp.max(s, axis=-1, keepdims=True))
    l = jnp.exp(s_new - m)
    o = l * vdn_row
    for p in range(n_pages):
        pr = jnp.exp(s_pages[p] - m)
        l = l + jnp.sum(pr, axis=-1, keepdims=True)
        o = o + _dot_nt(pr.astype(BF16), vd[p]().astype(BF16))
    o = o / l
    coef = jnp.where((lane8 >> 6) == (row8 >> 1), jnp.where((row8 & 1) == 0, 1.0, -lam), 0.0)
    res = jnp.sum(coef * o, axis=0, keepdims=True)
    od = _row_head_rms(res, gnd_ref[...], 1.0 - lam_init)

    qs = qs_row * (HEAD_W ** -0.5)
    qsm = jnp.where((lane8 >> 6) == row8, qs, 0.0).astype(BF16)
    z = jnp.concatenate([_dot(qsm, ks[p]().astype(BF16)) for p in range(n_pages)], axis=0)
    r = 8 * n_pages
    sp = _softplus(z)
    log_1m = -sp
    jj = lax.broadcasted_iota(jnp.int32, (page, page), 0)
    ss = lax.broadcasted_iota(jnp.int32, (page, page), 1)
    uincl = jnp.where(jj >= ss, 1.0, 0.0).astype(BF16)
    hi, lo = _split2(log_1m)
    incl = _dot(hi, uincl) + _dot(lo, uincl)
    tot = jnp.broadcast_to(jnp.sum(log_1m, axis=-1, keepdims=True), (r, page))
    ri = lax.broadcasted_iota(jnp.int32, (r, r), 0)
    ci = lax.broadcasted_iota(jnp.int32, (r, r), 1)
    upage = jnp.where(((ri & 7) == (ci & 7)) & ((ci >> 3) > (ri >> 3)), 1.0, 0.0).astype(BF16)
    thi, tlo = _split2(tot)
    carry = _dot(upage, thi) + _dot(upage, tlo)
    a = jnp.exp(z - sp + (incl - log_1m) + carry).astype(BF16)
    osb = None
    for p in range(n_pages):
        d = _dot_nt(a[8 * p:8 * p + 8, :], vs[p]().astype(BF16))
        osb = d if osb is None else osb + d
    res_s = jnp.sum(jnp.where((lane8 >> 6) == row8, osb, 0.0), axis=0, keepdims=True)
    return od, _row_head_rms(res_s, gns_ref[...], 1.0)


_DEC_SLOTS = 4


def _dec_attn_body(pt_ref, lq1, lk1, lq2, lk2, gnd_ref, gns_ref, qd_ref, kdn_ref, vdn_ref, qs_ref,
                   c0_ref, c1_ref, c2_ref, c3_ref, od_ref, os_ref, buf_ref, sem_ref,
                   *, layer, n_pages, page, lam_init, nb):
    i = pl.program_id(0)
    cache_refs = (c0_ref, c1_ref, c2_ref, c3_ref)

    def page_copies(seq, slot):
        return [pltpu.make_async_copy(cache_refs[c].at[layer, pt_ref[seq * n_pages + p]],
                                      buf_ref.at[slot, c, p], sem_ref.at[slot])
                for c in range(4) for p in range(n_pages)]

    def start(seq, slot):
        for n, cp in enumerate(page_copies(seq, slot)):
            cp.start(priority=n % 2)

    def wait(seq, slot):
        for cp in page_copies(seq, slot):
            cp.wait()

    lam = _lam(lq1, lk1, lq2, lk2, lam_init)

    def attend(r, slot):
        od, os_ = _dec_attn_one(lam, gnd_ref, gns_ref, qd_ref[r], kdn_ref[r], vdn_ref[r], qs_ref[r],
                                lambda c, p: buf_ref[slot, c, p], n_pages, page, lam_init)
        od_ref[r] = od
        os_ref[r] = os_

    @pl.when(i == 0)
    def _():
        start(0, 0)
        start(1, 1)

    for r in range(_DEC_SLOTS):
        seq = _DEC_SLOTS * i + r
        ahead = (r + 2) % _DEC_SLOTS

        @pl.when(seq + 2 < nb)
        def _():
            start(seq + 2, ahead)

        wait(seq, r)
        attend(r, r)


def _dec_attn(page_flat, lam_params, gnd, gns, qd, kdn, vdn, qs, caches, layer, n_pages, lam_init):
    nb = qd.shape[0]
    page = caches[0].shape[3]
    assert page == 128 and nb % _DEC_SLOTS == 0
    small = lambda a: pl.BlockSpec(a.shape, lambda b, pt: (0, 0))
    rowspec = pl.BlockSpec((_DEC_SLOTS, 1, ATT_W), lambda b, pt: (b, 0, 0))
    grid_spec = pltpu.PrefetchScalarGridSpec(
        num_scalar_prefetch=1, grid=(nb // _DEC_SLOTS,),
        in_specs=[small(a) for a in lam_params] + [small(gnd), small(gns), rowspec, rowspec, rowspec, rowspec]
        + [pl.BlockSpec(memory_space=pl.ANY)] * 4,
        out_specs=[rowspec, rowspec],
        scratch_shapes=[pltpu.VMEM((_DEC_SLOTS, 4, n_pages, ATT_W, page), F32),
                        pltpu.SemaphoreType.DMA((_DEC_SLOTS,))])
    return pl.pallas_call(
        functools.partial(_dec_attn_body, layer=layer, n_pages=n_pages, page=page, lam_init=lam_init, nb=nb),
        grid_spec=grid_spec,
        out_shape=[jax.ShapeDtypeStruct((nb, 1, ATT_W), F32)] * 2,
        compiler_params=_cparams("arbitrary"), name="dec_attn")(
            page_flat, *lam_params, gnd, gns, qd, kdn, vdn, qs, *caches)


def _split_cols(w):
    outs, i = [], 0
    for n in IN_SIZES:
        outs.append(w[:, i:i + n])
        i += n
    return outs


def _pack_w_in(w):
    dq, dk, dv, z, xbc, dt, sq, sk, sv = _split_cols(w)
    dt = jnp.pad(dt, ((0, 0), (0, DT_PAD - dt.shape[1])))
    w_nat = jnp.concatenate([z, xbc, dt, dk, sk, dq, sq, dv], axis=1).astype(BF16)
    w_t = jnp.concatenate([dk, dv, sk, sv, dq, sq], axis=1).T.astype(BF16)
    return w_nat, w_t


def kernel(x_prompt, x_sample, cache_diff_k, cache_diff_v, cache_sb_k, cache_sb_v, state_ssm, state_conv, page_table, norm_ffn1, w_ffn1_gu, w_ffn1_down, norm_mix, w_in, diff_lambda_q1, diff_lambda_k1, diff_lambda_q2, diff_lambda_k2, diff_norm, ssd_conv_w, ssd_conv_b, ssd_dt_bias, ssd_a_log, ssd_d, ssd_norm, sb_norm, w_out, norm_ffn2, w_ffn2_gu, w_ffn2_down, norm_final):
    batch, seq, _ = x_prompt.shape
    nb = x_sample.shape[0]
    depth = w_in.shape[0]
    n_pages = page_table.shape[1]
    n_pool, page = cache_diff_k.shape[1], cache_diff_k.shape[2]
    tp = batch * seq
    xp = x_prompt.reshape(tp, D_MODEL)
    xs = x_sample.reshape(nb, D_MODEL)
    caches = [jnp.transpose(c, (0, 1, 3, 4, 2)).reshape(depth, n_pool, ATT_W, page)
              for c in (cache_diff_k, cache_diff_v, cache_sb_k, cache_sb_v)]
    conv_in = jnp.transpose(state_conv, (0, 2, 1, 3))
    page_flat = page_table.reshape(-1).astype(jnp.int32)
    row = lambda a: a.reshape(1, -1).astype(F32)
    gfinal = row(norm_final)
    new_p, new_s = [], []
    kv_p = kv_s = h_s = None
    for l in range(depth):
        lam_init = 0.8 - 0.6 * math.exp(-0.3 * l)
        last = l == depth - 1
        wg1, wu1 = w_ffn1_gu[l][:, :D_FF].astype(BF16), w_ffn1_gu[l][:, D_FF:].astype(BF16)
        wd1 = w_ffn1_down[l].astype(BF16)
        wg2, wu2 = w_ffn2_gu[l][:, :D_FF].astype(BF16), w_ffn2_gu[l][:, D_FF:].astype(BF16)
        wd2 = w_ffn2_down[l].astype(BF16)
        w_nat, w_t = _pack_w_in(w_in[l])
        w_out_b = w_out[l].astype(BF16)
        lam_params = [row(a[l]) for a in (diff_lambda_q1, diff_lambda_k1, diff_lambda_q2, diff_lambda_k2)]
        gnd = row(jnp.tile(diff_norm[l], 4))
        gns = row(jnp.tile(sb_norm[l], 4))
        cw, cb = ssd_conv_w[l].astype(F32), row(ssd_conv_b[l])
        dtb = row(jnp.pad(ssd_dt_bias[l], (0, DT_PAD - SSD_HEADS)))
        alog = row(jnp.pad(ssd_a_log[l], (0, DT_PAD - SSD_HEADS)))
        dfull = row(jnp.repeat(ssd_d[l], HEAD_W))
        nw = row(ssd_norm[l])
        g1, gm, g2 = row(norm_ffn1[l]), row(norm_mix[l]), row(norm_ffn2[l])

        xp = _ffn(xp, g1, wg1, wu1, wd1)
        z, xbc, dt, *rest = _inproj(xp, gm, w_nat, w_t, batch, seq, False, l, depth, kv_p)
        kv_p, (dq_t, sq_t, dk_a, sk_a, dv_a, sv_a) = rest[:4], rest[4:]
        od = _diff_attn(dq_t, dk_a, dv_a, lam_params, gnd.reshape(ATT_W, 1), batch, seq, lam_init)
        os_ = _sb_attn(sq_t, sk_a, sv_a, gns.reshape(ATT_W, 1), batch, seq)
        y, h_p, conv_p = _ssd_prompt(z, xbc, dt, cw, cb, dtb, alog, dfull, nw, batch, seq)
        xp = _ffn(xp, g2, wg2, wu2, wd2, gfinal if last else None, mix=(od, y, os_, w_out_b))
        new_p.append((h_p, conv_p))

        xs = _ffn(xs, g1, wg1, wu1, wd1)
        z, xbc, dt, *rest = _inproj(xs, gm, w_nat, w_t, 1, nb, True, l, depth, kv_s)
        kv_s, (dq, sq, dk_n, dv_n) = rest[:4], rest[4:]
        r3 = lambda a: a.reshape(nb, 1, ATT_W)
        od, os_ = _dec_attn(page_flat, lam_params, gnd, gns, r3(dq), r3(dk_n), r3(dv_n), r3(sq), caches, l,
                            n_pages, lam_init)
        y, h_s, conv_s = _ssd_sample(z, xbc, dt, conv_in, state_ssm, l, cw, cb, dtb, alog, dfull, nw, h_s)
        xs = _ffn(xs, g2, wg2, wu2, wd2, gfinal if last else None,
                  mix=(od.reshape(nb, ATT_W), y, os_.reshape(nb, ATT_W), w_out_b))
        new_s.append(conv_s)

    kv_p_out = [jnp.transpose(a.reshape(depth, batch, 4, HEAD_W, seq), (0, 1, 4, 2, 3)) for a in kv_p]
    kv_s_out = [jnp.transpose(a.reshape(depth, 4, HEAD_W, nb), (0, 3, 1, 2)).reshape(depth, nb, 1, 4, HEAD_W)
                for a in kv_s]
    conv_s_out = jnp.transpose(jnp.stack(new_s, axis=0), (0, 2, 1, 3))
    return (xp.reshape(batch, seq, D_MODEL), xs.reshape(nb, 1, D_MODEL), *kv_p_out,
            jnp.stack([p[0] for p in new_p], axis=0), jnp.stack([p[1] for p in new_p], axis=0),
            *kv_s_out, h_s, conv_s_out)
```
